```python
import jax, jax.numpy as jnp
from jax import lax
import numpy as np

D_MODEL = 1024
BATCH = 2
SEQ = 16384
DEPTH = 4
DEC_BATCH = 2
DEC_SEQ = 8192
PAST_LEN = 128

D_MIX = D_MODEL
HEAD_DIM = 64
N_HEADS = (D_MIX // 2) // HEAD_DIM
N_KV_HEADS = 2
GROUP = N_HEADS // N_KV_HEADS
ATTN_W = N_HEADS * HEAD_DIM
KV_W = N_KV_HEADS * HEAD_DIM
SG_W = D_MIX - ATTN_W
SG_GROUPS = 8
SG_GROUP_DIM = SG_W // SG_GROUPS
CHUNK = 128
Q_BLOCK = 128
GRID_W = 64
AXIS_DIM = HEAD_DIM // 2
ROPE_THETA = 10000.0
D_FF = 4 * D_MODEL
D_IN = ATTN_W + 2 * KV_W + 2 * SG_W
EPS = 1e-6

kernel_name = "hymba_gmlp_axial_gqa_encoder"


def rms_norm(x, g):
    xf = x.astype(jnp.float32)
    y = xf * lax.rsqrt(jnp.mean(xf * xf, axis=-1, keepdims=True) + EPS)
    return (y * g.astype(jnp.float32)).astype(x.dtype)


def layer_norm(x, g, b):
    xf = x.astype(jnp.float32)
    mu = jnp.mean(xf, axis=-1, keepdims=True)
    xc = xf - mu
    y = xc * lax.rsqrt(jnp.mean(xc * xc, axis=-1, keepdims=True) + EPS)
    return (y * g.astype(jnp.float32) + b.astype(jnp.float32)).astype(x.dtype)


def axial_rope_tables(seq_len):
    rows = seq_len // GRID_W
    r = jnp.broadcast_to(jnp.arange(rows)[:, None], (rows, GRID_W)).reshape(-1).astype(jnp.float32)
    c = jnp.broadcast_to(jnp.arange(GRID_W)[None, :], (rows, GRID_W)).reshape(-1).astype(jnp.float32)
    inv = ROPE_THETA ** (-jnp.arange(0, AXIS_DIM, 2, dtype=jnp.float32) / AXIS_DIM)
    ang_r = r[:, None] * inv
    ang_c = c[:, None] * inv
    ang = jnp.concatenate([ang_r, ang_r, ang_c, ang_c], axis=-1)
    return jnp.cos(ang)[None, :, None, :], jnp.sin(ang)[None, :, None, :]


def apply_axial_rope(x, cos, sin):
    xf = x.astype(jnp.float32)
    x4 = xf.reshape(xf.shape[:-1] + (2, 2, AXIS_DIM // 2))
    rot = jnp.stack([-x4[..., 1, :], x4[..., 0, :]], axis=-2).reshape(xf.shape)
    return (xf * cos + rot * sin).astype(x.dtype)


def gqa_axial(q, k, v, q_g, k_g):
    B, S, _ = q.shape
    q = q.reshape(B, S, N_HEADS, HEAD_DIM)
    k = k.reshape(B, S, N_KV_HEADS, HEAD_DIM)
    v = v.reshape(B, S, N_KV_HEADS, HEAD_DIM)
    cos, sin = axial_rope_tables(S)
    q = apply_axial_rope(rms_norm(q, q_g), cos, sin) * (HEAD_DIM ** -0.5)
    k = apply_axial_rope(rms_norm(k, k_g), cos, sin)
    nb = S // Q_BLOCK
    qb = q.reshape(B, nb, Q_BLOCK, N_KV_HEADS, GROUP, HEAD_DIM).transpose(1, 0, 2, 3, 4, 5)

    def one_block(qi):
        s = jnp.einsum('bqkgd,bskd->bkgqs', qi, k).astype(jnp.float32)
        p = jax.nn.softmax(s, axis=-1).astype(v.dtype)
        return jnp.einsum('bkgqs,bskd->bqkgd', p, v)

    o = lax.map(one_block, qb)
    return o.transpose(1, 0, 2, 3, 4, 5).reshape(B, S, ATTN_W)


def chunked_spatial_gating(z, ln_g, ln_b, w_s, b_s):
    B, S, _ = z.shape
    z = jax.nn.gelu(z)
    u, vv = z[..., :SG_W], z[..., SG_W:]
    vv = layer_norm(vv, ln_g, ln_b)
    vc = vv.reshape(B, S // CHUNK, CHUNK, SG_GROUPS, SG_GROUP_DIM)
    mixed = jnp.einsum('gij,bnjgc->bnigc', w_s, vc) + b_s.T[None, None, :, :, None]
    return u * mixed.reshape(B, S, SG_W)


def trunk(x, w_in, w_out, q_norm_g, k_norm_g, sg_norm_g, sg_norm_b, sg_w, sg_b,
          attn_out_g, sg_out_g, pre_mix_g, post_mix_g, pre_ffn_g, post_ffn_g, w_ff1, w_ff2):
    for l in range(DEPTH):
        h = rms_norm(x, pre_mix_g[l])
        z = h @ w_in[l]
        q = z[..., :ATTN_W]
        k = z[..., ATTN_W:ATTN_W + KV_W]
        v = z[..., ATTN_W + KV_W:ATTN_W + 2 * KV_W]
        zs = z[..., ATTN_W + 2 * KV_W:]
        a = gqa_axial(q, k, v, q_norm_g[l], k_norm_g[l])
        g = chunked_spatial_gating(zs, sg_norm_g[l], sg_norm_b[l], sg_w[l], sg_b[l])
        m = jnp.concatenate([rms_norm(a, attn_out_g[l]), rms_norm(g, sg_out_g[l])], axis=-1) @ w_out[l]
        x = x + rms_norm(m, post_mix_g[l])
        h = rms_norm(x, pre_ffn_g[l])
        f = jnp.square(jax.nn.relu(h @ w_ff1[l])) @ w_ff2[l]
        x = x + rms_norm(f, post_ffn_g[l])
    return x


def setup_inputs(seed: int = 0) -> dict:
    key = jax.random.key(seed)
    ks = jax.random.split(key, 20)
    f32 = jnp.float32

    def nrm(k, shape, scale):
        return jax.random.normal(k, shape, f32) * scale

    def gain(k, shape):
        return 1.0 + 0.02 * jax.random.normal(k, shape, f32)

    return {
        "x_prompt": jax.random.normal(ks[0], (BATCH, SEQ, D_MODEL), f32),
        "x_sample": jax.random.normal(ks[1], (DEC_BATCH, DEC_SEQ, D_MODEL), f32),
        "w_in": nrm(ks[2], (DEPTH, D_MODEL, D_IN), D_MODEL ** -0.5),
        "w_out": nrm(ks[3], (DEPTH, D_MIX, D_MODEL), D_MIX ** -0.5),
        "q_norm_g": gain(ks[4], (DEPTH, HEAD_DIM)),
        "k_norm_g": gain(ks[5], (DEPTH, HEAD_DIM)),
        "sg_norm_g": gain(ks[6], (DEPTH, SG_W)),
        "sg_norm_b": nrm(ks[7], (DEPTH, SG_W), 0.02),
        "sg_w": nrm(ks[8], (DEPTH, SG_GROUPS, CHUNK, CHUNK), CHUNK ** -0.5),
        "sg_b": gain(ks[9], (DEPTH, SG_GROUPS, CHUNK)),
        "attn_out_g": gain(ks[10], (DEPTH, ATTN_W)),
        "sg_out_g": gain(ks[11], (DEPTH, SG_W)),
        "pre_mix_g": gain(ks[12], (DEPTH, D_MODEL)),
        "post_mix_g": gain(ks[13], (DEPTH, D_MODEL)),
        "pre_ffn_g": gain(ks[14], (DEPTH, D_MODEL)),
        "post_ffn_g": gain(ks[15], (DEPTH, D_MODEL)),
        "w_ff1": nrm(ks[16], (DEPTH, D_MODEL, D_FF), D_MODEL ** -0.5),
        "w_ff2": nrm(ks[17], (DEPTH, D_FF, D_MODEL), D_FF ** -0.5),
    }


def reference(x_prompt, x_sample, w_in, w_out, q_norm_g, k_norm_g, sg_norm_g, sg_norm_b,
              sg_w, sg_b, attn_out_g, sg_out_g, pre_mix_g, post_mix_g, pre_ffn_g,
              post_ffn_g, w_ff1, w_ff2):
    y_prompt = trunk(x_prompt, w_in, w_out, q_norm_g, k_norm_g, sg_norm_g, sg_norm_b, sg_w, sg_b,
                     attn_out_g, sg_out_g, pre_mix_g, post_mix_g, pre_ffn_g, post_ffn_g, w_ff1, w_ff2)
    y_sample = trunk(x_sample, w_in, w_out, q_norm_g, k_norm_g, sg_norm_g, sg_norm_b, sg_w, sg_b,
                     attn_out_g, sg_out_g, pre_mix_g, post_mix_g, pre_ffn_g, post_ffn_g, w_ff1, w_ff2)
    return (y_prompt, y_sample)
```

```python
import functools

import jax
import jax.numpy as jnp
from jax import lax
from jax.experimental import pallas as pl
from jax.experimental.pallas import tpu as pltpu

D_MODEL = 1024
HEAD_DIM = 64
N_HEADS = 8
N_KV_HEADS = 2
GROUP = N_HEADS // N_KV_HEADS
ATTN_W = N_HEADS * HEAD_DIM
KV_W = N_KV_HEADS * HEAD_DIM
QKV_W = ATTN_W + 2 * KV_W
SG_W = 512
SG_GROUPS = 8
SG_PAIRS = SG_GROUPS // 2
CHUNK = 128
GRID_W = 64
AXIS_DIM = HEAD_DIM // 2
ROPE_THETA = 10000.0
D_FF = 4 * D_MODEL
D_IN = QKV_W + 2 * SG_W
EPS = 1e-6

LANES = 128
BF16_SUBLANES = 16
VT_ROWS = HEAD_DIM + BF16_SUBLANES

ROW_TILE = 512
Q_TILE = 256
FF_CHUNK = 1024
VMEM_LIMIT = 56 * 1024 * 1024

BF = jnp.bfloat16
F32 = jnp.float32


def _rms(x, g):
    ms = jnp.mean(x * x, axis=-1, keepdims=True)
    return x * lax.rsqrt(ms + EPS) * g


def _gelu_tanh(x):
    c = 0.7978845608028654
    return 0.5 * x * (1.0 + jnp.tanh(c * (x + 0.044715 * (x * x * x))))


def _const_spec(shape):
    nd = len(shape)
    return pl.BlockSpec(shape, lambda *_: (0,) * nd, pipeline_mode=pl.Buffered(1))


def _proj_kernel(x_ref, g_ref, w_ref, qg_ref, kg_ref, cos_ref, sin_ref, lng_ref, lnb_ref, ws_ref, bs_ref,
                 sgo_ref, qt_ref, k_ref, vt_ref, gn_ref):
    x = x_ref[0]
    tm = x.shape[0]
    h = _rms(x, g_ref[...]).astype(BF)
    z = jnp.dot(h, w_ref[...], preferred_element_type=F32)

    zt = z[:, :QKV_W].T
    cos = cos_ref[...]
    sin = sin_ref[...]

    def norm_rope(blk, gcol):
        ms = jnp.mean(blk * blk, axis=0, keepdims=True)
        y = blk * lax.rsqrt(ms + EPS) * gcol
        half = AXIS_DIM // 2
        sw = jnp.concatenate([y[half:2 * half], y[0:half], y[3 * half:4 * half], y[2 * half:3 * half]], axis=0)
        return y * cos + sw * sin

    zero_blk = jnp.zeros((HEAD_DIM, tm), BF)
    for hh in range(N_HEADS):
        qh = norm_rope(zt[hh * HEAD_DIM:(hh + 1) * HEAD_DIM], qg_ref[...]) * (HEAD_DIM ** -0.5)
        c = hh // GROUP
        for cc in range(N_KV_HEADS):
            qt_ref[0, hh, cc * HEAD_DIM:(cc + 1) * HEAD_DIM, :] = qh.astype(BF) if cc == c else zero_blk
    kt = jnp.concatenate(
        [norm_rope(zt[ATTN_W + c * HEAD_DIM:ATTN_W + (c + 1) * HEAD_DIM], kg_ref[...]) for c in range(N_KV_HEADS)],
        axis=0)
    k_ref[0] = kt.T.astype(BF)
    ones_blk = jnp.ones((BF16_SUBLANES, tm), BF)
    for c in range(N_KV_HEADS):
        v0 = ATTN_W + KV_W + c * HEAD_DIM
        vt_ref[0, c, 0, 0:HEAD_DIM, :] = zt[v0:v0 + HEAD_DIM].astype(BF)
        vt_ref[0, c, 0, HEAD_DIM:VT_ROWS, :] = ones_blk

    gz = _gelu_tanh(z[:, QKV_W:])
    u = gz[:, :SG_W]
    vv = gz[:, SG_W:]
    mu = jnp.mean(vv, axis=-1, keepdims=True)
    xc = vv - mu
    var = jnp.mean(xc * xc, axis=-1, keepdims=True)
    vb = (xc * lax.rsqrt(var + EPS) * lng_ref[...] + lnb_ref[...]).astype(BF)
    nch = tm // CHUNK
    low_half = lax.broadcasted_iota(jnp.int32, (CHUNK, LANES), 1) < SG_W // SG_GROUPS
    pair_cols = []
    for j in range(SG_PAIRS):
        rhs = jnp.concatenate(
            [vb[c * CHUNK:(c + 1) * CHUNK, j * LANES:(j + 1) * LANES] for c in range(nch)], axis=1)
        r = jnp.dot(ws_ref[j], rhs, preferred_element_type=F32)
        pair_cols.append(jnp.concatenate(
            [jnp.where(low_half, r[:CHUNK, c * LANES:(c + 1) * LANES], r[CHUNK:, c * LANES:(c + 1) * LANES])
             for c in range(nch)], axis=0))
    bias = jnp.concatenate([bs_ref[...]] * nch, axis=0)
    mixed = jnp.concatenate(pair_cols, axis=1) + bias
    gn_ref[0] = _rms(u * mixed, sgo_ref[...]).astype(BF)


def _proj_call(x, g, w, qg, kg, cos_t, sin_t, lng, lnb, ws, bs, sgo):
    B, S, _ = x.shape
    tm = min(ROW_TILE, S)
    n = S // tm
    out_shape = (
        jax.ShapeDtypeStruct((B, N_HEADS, KV_W, S), BF),
        jax.ShapeDtypeStruct((B, S, KV_W), BF),
        jax.ShapeDtypeStruct((B, N_KV_HEADS, n, VT_ROWS, tm), BF),
        jax.ShapeDtypeStruct((B, S, SG_W), BF),
    )
    return pl.pallas_call(
        _proj_kernel,
        grid=(B, n),
        in_specs=[
            pl.BlockSpec((1, tm, D_MODEL), lambda b, i: (b, i, 0)),
            _const_spec((1, D_MODEL)),
            _const_spec((D_MODEL, D_IN)),
            _const_spec((HEAD_DIM, 1)),
            _const_spec((HEAD_DIM, 1)),
            pl.BlockSpec((HEAD_DIM, tm), lambda b, i: (0, i)),
            pl.BlockSpec((HEAD_DIM, tm), lambda b, i: (0, i)),
            _const_spec((1, SG_W)),
            _const_spec((1, SG_W)),
            _const_spec((SG_PAIRS, 2 * CHUNK, CHUNK)),
            _const_spec((CHUNK, SG_W)),
            _const_spec((1, SG_W)),
        ],
        out_specs=(
            pl.BlockSpec((1, N_HEADS, KV_W, tm), lambda b, i: (b, 0, 0, i)),
            pl.BlockSpec((1, tm, KV_W), lambda b, i: (b, i, 0)),
            pl.BlockSpec((1, N_KV_HEADS, 1, VT_ROWS, tm), lambda b, i: (b, 0, i, 0, 0)),
            pl.BlockSpec((1, tm, SG_W), lambda b, i: (b, i, 0)),
        ),
        out_shape=out_shape,
        compiler_params=pltpu.CompilerParams(
            dimension_semantics=("parallel", "parallel"), vmem_limit_bytes=VMEM_LIMIT),
        name="proj",
    )(x, g, w, qg, kg, cos_t, sin_t, lng, lnb, ws, bs, sgo)


def _attn_kernel(qt_ref, k_ref, vt_ref, ag_ref, o_ref, m_ref, acc_ref):
    nkv = vt_ref.shape[2]
    tk = vt_ref.shape[4]
    m_ref[...] = jnp.full(m_ref.shape, -jnp.inf, F32)
    acc_ref[...] = jnp.zeros(acc_ref.shape, F32)

    def body(j, carry):
        start = pl.multiple_of(j * tk, tk)
        kc = k_ref[0, pl.ds(start, tk), :]
        for h in range(N_HEADS):
            st = jnp.dot(kc, qt_ref[0, h], preferred_element_type=F32)
            m_old = m_ref[h]
            m_new = jnp.maximum(m_old, jnp.max(st, axis=0, keepdims=True))
            alpha = jnp.exp(m_old - m_new)
            p = jnp.exp(st - m_new).astype(BF)
            pv = jnp.dot(vt_ref[0, h // GROUP, j], p, preferred_element_type=F32)
            acc_ref[h] = alpha * acc_ref[h] + pv
            m_ref[h] = m_new
        return carry

    lax.fori_loop(0, nkv, body, 0)

    outs = []
    for h in range(N_HEADS):
        a = acc_ref[h]
        outs.append(a[:HEAD_DIM] / a[HEAD_DIM:HEAD_DIM + 1])
    o = jnp.concatenate(outs, axis=0).T
    o_ref[0] = _rms(o, ag_ref[...]).astype(BF)


def _attn_call(qt, k, vt, ag):
    B, _, _, S = qt.shape
    nkv, tk = vt.shape[2], vt.shape[4]
    tq = min(Q_TILE, S)
    return pl.pallas_call(
        _attn_kernel,
        grid=(B, S // tq),
        in_specs=[
            pl.BlockSpec((1, N_HEADS, KV_W, tq), lambda b, i: (b, 0, 0, i)),
            pl.BlockSpec((1, S, KV_W), lambda b, i: (b, 0, 0)),
            pl.BlockSpec((1, N_KV_HEADS, nkv, VT_ROWS, tk), lambda b, i: (b, 0, 0, 0, 0)),
            _const_spec((1, ATTN_W)),
        ],
        out_specs=pl.BlockSpec((1, tq, ATTN_W), lambda b, i: (b, i, 0)),
        out_shape=jax.ShapeDtypeStruct((B, S, ATTN_W), BF),
        scratch_shapes=[
            pltpu.VMEM((N_HEADS, 1, tq), F32),
            pltpu.VMEM((N_HEADS, VT_ROWS, tq), F32),
        ],
        compiler_params=pltpu.CompilerParams(
            dimension_semantics=("parallel", "arbitrary"), vmem_limit_bytes=VMEM_LIMIT),
        name="attn",
    )(qt, k, vt, ag)


def _mix_ffn_kernel(x_ref, an_ref, gn_ref, wo_ref, pmg_ref, pfg_ref, w1_ref, w2_ref, pog_ref, o_ref):
    cat = jnp.concatenate([an_ref[...], gn_ref[...]], axis=-1)
    m = jnp.dot(cat, wo_ref[...], preferred_element_type=F32)
    x1 = x_ref[...] + _rms(m, pmg_ref[...])
    h = _rms(x1, pfg_ref[...]).astype(BF)
    f = None
    for c in range(D_FF // FF_CHUNK):
        a = jnp.dot(h, w1_ref[:, c * FF_CHUNK:(c + 1) * FF_CHUNK], preferred_element_type=F32)
        a = jnp.maximum(a, 0.0)
        a = (a * a).astype(BF)
        part = jnp.dot(a, w2_ref[c * FF_CHUNK:(c + 1) * FF_CHUNK, :], preferred_element_type=F32)
        f = part if f is None else f + part
    o_ref[...] = x1 + _rms(f, pog_ref[...])


def _mix_ffn_call(x, an, gn, wo, pmg, pfg, w1, w2, pog):
    n_rows = x.shape[0]
    tm = min(ROW_TILE, n_rows)
    row = lambda width: pl.BlockSpec((tm, width), lambda i: (i, 0))
    return pl.pallas_call(
        _mix_ffn_kernel,
        grid=(n_rows // tm,),
        in_specs=[
            row(D_MODEL), row(ATTN_W), row(SG_W),
            _const_spec((D_MODEL, D_MODEL)),
            _const_spec((1, D_MODEL)),
            _const_spec((1, D_MODEL)),
            _const_spec((D_MODEL, D_FF)),
            _const_spec((D_FF, D_MODEL)),
            _const_spec((1, D_MODEL)),
        ],
        out_specs=row(D_MODEL),
        out_shape=jax.ShapeDtypeStruct((n_rows, D_MODEL), F32),
        compiler_params=pltpu.CompilerParams(
            dimension_semantics=("parallel",), vmem_limit_bytes=VMEM_LIMIT),
        name="mix_ffn",
    )(x, an, gn, wo, pmg, pfg, w1, w2, pog)


def _rope_tables_t(seq_len):
    pos = jnp.arange(seq_len)
    r = (pos // GRID_W).astype(F32)
    c = (pos % GRID_W).astype(F32)
    inv = ROPE_THETA ** (-jnp.arange(0, AXIS_DIM, 2, dtype=F32) / AXIS_DIM)
    ang_r = inv[:, None] * r[None, :]
    ang_c = inv[:, None] * c[None, :]
    ang = jnp.concatenate([ang_r, ang_r, ang_c, ang_c], axis=0)
    half = AXIS_DIM // 2
    sign = jnp.concatenate([-jnp.ones((half, 1), F32), jnp.ones((half, 1), F32)] * 2, axis=0)
    return jnp.cos(ang), jnp.sin(ang) * sign


def _trunk(x, p):
    B, S, _ = x.shape
    cos_t, sin_t = _rope_tables_t(S)
    depth = p["w_in"].shape[0]
    for l in range(depth):
        qt, k, vt, gn = _proj_call(
            x, p["pre_mix_g"][l][None, :], p["w_in"][l], p["q_norm_g"][l][:, None], p["k_norm_g"][l][:, None],
            cos_t, sin_t, p["sg_norm_g"][l][None, :], p["sg_norm_b"][l][None, :], p["sg_w"][l], p["sg_b"][l],
            p["sg_out_g"][l][None, :])
        an = _attn_call(qt, k, vt, p["attn_out_g"][l][None, :])
        x = _mix_ffn_call(
            x.reshape(B * S, D_MODEL), an.reshape(B * S, ATTN_W), gn.reshape(B * S, SG_W), p["w_out"][l],
            p["post_mix_g"][l][None, :], p["pre_ffn_g"][l][None, :], p["w_ff1"][l], p["w_ff2"][l],
            p["post_ffn_g"][l][None, :]).reshape(B, S, D_MODEL)
    return x


def kernel(x_prompt, x_sample, w_in, w_out, q_norm_g, k_norm_g, sg_norm_g, sg_norm_b, sg_w, sg_b, attn_out_g,
           sg_out_g, pre_mix_g, post_mix_g, pre_ffn_g, post_ffn_g, w_ff1, w_ff2):
    depth = w_in.shape[0]
    p = dict(
        w_in=w_in.astype(BF), w_out=w_out.astype(BF), w_ff1=w_ff1.astype(BF), w_ff2=w_ff2.astype(BF),
        q_norm_g=q_norm_g, k_norm_g=k_norm_g, sg_norm_g=sg_norm_g, sg_norm_b=sg_norm_b,
        sg_w=sg_w.astype(BF).reshape(depth, SG_PAIRS, 2 * CHUNK, CHUNK),
        sg_b=jnp.repeat(jnp.swapaxes(sg_b, 1, 2), SG_W // SG_GROUPS, axis=2),
        attn_out_g=attn_out_g, sg_out_g=sg_out_g, pre_mix_g=pre_mix_g, post_mix_g=post_mix_g,
        pre_ffn_g=pre_ffn_g, post_ffn_g=post_ffn_g,
    )
    return (_trunk(x_prompt, p), _trunk(x_sample, p))
```

```python
import functools

import jax
import jax.numpy as jnp
from jax import lax
from jax.experimental import pallas as pl
from jax.experimental.pallas import tpu as pltpu

D_MODEL = 1024
HEAD_DIM = 64
N_HEADS = 8
N_KV_HEADS = 2
GROUP = N_HEADS // N_KV_HEADS
ATTN_W = N_HEADS * HEAD_DIM
KV_W = N_KV_HEADS * HEAD_DIM
QKV_W = ATTN_W + 2 * KV_W
SG_W = 512
SG_GROUPS = 8
SG_PAIRS = SG_GROUPS // 2
CHUNK = 128
GRID_W = 64
AXIS_DIM = HEAD_DIM // 2
ROPE_THETA = 10000.0
D_FF = 4 * D_MODEL
D_IN = QKV_W + 2 * SG_W
EPS = 1e-6

LANES = 128
BF16_SUBLANES = 16
VT_ROWS = HEAD_DIM + BF16_SUBLANES

ROW_TILE = 512
Q_TILE = 256
FF_CHUNK = 1024
S_AHEAD = 3
KV_UNROLL = 4
LOG2_E = 1.4426950408889634
MIN_DENOM = 2.0 ** -60
VMEM_LIMIT = 56 * 1024 * 1024

BF = jnp.bfloat16
F32 = jnp.float32


def _rms(x, g):
    ms = jnp.mean(x * x, axis=-1, keepdims=True)
    return x * lax.rsqrt(ms + EPS) * g


def _gelu_tanh(x):
    c = 0.7978845608028654
    return 0.5 * x * (1.0 + jnp.tanh(c * (x + 0.044715 * (x * x * x))))


def _const_spec(shape):
    nd = len(shape)
    return pl.BlockSpec(shape, lambda *_: (0,) * nd, pipeline_mode=pl.Buffered(1))


def _proj_kernel(x_ref, g_ref, w_ref, qg_ref, kg_ref, cos_ref, sin_ref, lng_ref, lnb_ref, ws_ref, bs_ref,
                 sgo_ref, qt_ref, k_ref, ksq_ref, vt_ref, gn_ref):
    x = x_ref[0]
    tm = x.shape[0]
    h = _rms(x, g_ref[...]).astype(BF)
    z = jnp.dot(h, w_ref[...], preferred_element_type=F32)

    zt = z[:, :QKV_W].T
    cos = cos_ref[...]
    sin = sin_ref[...]

    def norm_rope(blk, gcol):
        ms = jnp.mean(blk * blk, axis=0, keepdims=True)
        y = blk * lax.rsqrt(ms + EPS) * gcol
        half = AXIS_DIM // 2
        sw = jnp.concatenate([y[half:2 * half], y[0:half], y[3 * half:4 * half], y[2 * half:3 * half]], axis=0)
        return y * cos + sw * sin

    zero_blk = jnp.zeros((HEAD_DIM, tm), BF)
    for hh in range(N_HEADS):
        qh = norm_rope(zt[hh * HEAD_DIM:(hh + 1) * HEAD_DIM], qg_ref[...]) * (HEAD_DIM ** -0.5 * LOG2_E)
        c = hh // GROUP
        for cc in range(N_KV_HEADS):
            qt_ref[0, hh, cc * HEAD_DIM:(cc + 1) * HEAD_DIM, :] = qh.astype(BF) if cc == c else zero_blk
    kt = jnp.concatenate(
        [norm_rope(zt[ATTN_W + c * HEAD_DIM:ATTN_W + (c + 1) * HEAD_DIM], kg_ref[...]) for c in range(N_KV_HEADS)],
        axis=0)
    k_ref[0] = kt.T.astype(BF)
    kf = kt.astype(BF).astype(F32)
    kf = kf * kf
    ksq_ref[0] = jnp.concatenate(
        [jnp.sum(kf[c * HEAD_DIM:(c + 1) * HEAD_DIM], axis=0, keepdims=True) for c in range(N_KV_HEADS)], axis=0)
    ones_blk = jnp.ones((BF16_SUBLANES, tm), BF)
    for c in range(N_KV_HEADS):
        v0 = ATTN_W + KV_W + c * HEAD_DIM
        vt_ref[0, c, 0, 0:HEAD_DIM, :] = zt[v0:v0 + HEAD_DIM].astype(BF)
        vt_ref[0, c, 0, HEAD_DIM:VT_ROWS, :] = ones_blk

    gz = _gelu_tanh(z[:, QKV_W:])
    u = gz[:, :SG_W]
    vv = gz[:, SG_W:]
    mu = jnp.mean(vv, axis=-1, keepdims=True)
    xc = vv - mu
    var = jnp.mean(xc * xc, axis=-1, keepdims=True)
    vb = (xc * lax.rsqrt(var + EPS) * lng_ref[...] + lnb_ref[...]).astype(BF)
    nch = tm // CHUNK
    low_half = lax.broadcasted_iota(jnp.int32, (CHUNK, LANES), 1) < SG_W // SG_GROUPS
    pair_cols = []
    for j in range(SG_PAIRS):
        rhs = jnp.concatenate(
            [vb[c * CHUNK:(c + 1) * CHUNK, j * LANES:(j + 1) * LANES] for c in range(nch)], axis=1)
        r = jnp.dot(ws_ref[j], rhs, preferred_element_type=F32)
        pair_cols.append(jnp.concatenate(
            [jnp.where(low_half, r[:CHUNK, c * LANES:(c + 1) * LANES], r[CHUNK:, c * LANES:(c + 1) * LANES])
             for c in range(nch)], axis=0))
    bias = jnp.concatenate([bs_ref[...]] * nch, axis=0)
    mixed = jnp.concatenate(pair_cols, axis=1) + bias
    gn_ref[0] = _rms(u * mixed, sgo_ref[...]).astype(BF)


def _proj_call(x, g, w, qg, kg, cos_t, sin_t, lng, lnb, ws, bs, sgo):
    B, S, _ = x.shape
    tm = min(ROW_TILE, S)
    n = S // tm
    out_shape = (
        jax.ShapeDtypeStruct((B, N_HEADS, KV_W, S), BF),
        jax.ShapeDtypeStruct((B, S, KV_W), BF),
        jax.ShapeDtypeStruct((B, N_KV_HEADS, S), F32),
        jax.ShapeDtypeStruct((B, N_KV_HEADS, n, VT_ROWS, tm), BF),
        jax.ShapeDtypeStruct((B, S, SG_W), BF),
    )
    return pl.pallas_call(
        _proj_kernel,
        grid=(B, n),
        in_specs=[
            pl.BlockSpec((1, tm, D_MODEL), lambda b, i: (b, i, 0)),
            _const_spec((1, D_MODEL)),
            _const_spec((D_MODEL, D_IN)),
            _const_spec((HEAD_DIM, 1)),
            _const_spec((HEAD_DIM, 1)),
            pl.BlockSpec((HEAD_DIM, tm), lambda b, i: (0, i)),
            pl.BlockSpec((HEAD_DIM, tm), lambda b, i: (0, i)),
            _const_spec((1, SG_W)),
            _const_spec((1, SG_W)),
            _const_spec((SG_PAIRS, 2 * CHUNK, CHUNK)),
            _const_spec((CHUNK, SG_W)),
            _const_spec((1, SG_W)),
        ],
        out_specs=(
            pl.BlockSpec((1, N_HEADS, KV_W, tm), lambda b, i: (b, 0, 0, i)),
            pl.BlockSpec((1, tm, KV_W), lambda b, i: (b, i, 0)),
            pl.BlockSpec((1, N_KV_HEADS, tm), lambda b, i: (b, 0, i)),
            pl.BlockSpec((1, N_KV_HEADS, 1, VT_ROWS, tm), lambda b, i: (b, 0, i, 0, 0)),
            pl.BlockSpec((1, tm, SG_W), lambda b, i: (b, i, 0)),
        ),
        out_shape=out_shape,
        compiler_params=pltpu.CompilerParams(
            dimension_semantics=("parallel", "parallel"), vmem_limit_bytes=VMEM_LIMIT),
        name="proj",
    )(x, g, w, qg, kg, cos_t, sin_t, lng, lnb, ws, bs, sgo)


def _attn_kernel(qt_ref, k_ref, ksq_ref, vt_ref, ag_ref, o_ref, stab_ref, acc_ref):
    nkv = vt_ref.shape[2]
    tk = vt_ref.shape[4]
    unroll = KV_UNROLL if nkv % KV_UNROLL == 0 else 1

    def k_chunk(j):
        return k_ref[0, pl.ds(pl.multiple_of(j * tk, tk), tk), :]

    def sweep():
        acc_ref[...] = jnp.zeros(acc_ref.shape, F32)

        def body(jj, carry):
            items = [(jj * unroll + c, h) for c in range(unroll) for h in range(N_HEADS)]
            score = lambda j, h: jnp.dot(k_chunk(j), qt_ref[0, h], preferred_element_type=F32)
            ahead = [score(*it) for it in items[:S_AHEAD]]
            for n, (j, h) in enumerate(items):
                if n + S_AHEAD < len(items):
                    ahead.append(score(*items[n + S_AHEAD]))
                st = ahead.pop(0)
                p = jnp.exp2(st - stab_ref[h]).astype(BF)
                acc_ref[h] += jnp.dot(vt_ref[0, h // GROUP, j], p, preferred_element_type=F32)
            return carry

        lax.fori_loop(0, nkv // unroll, body, 0)
        return jnp.min(acc_ref[:, HEAD_DIM:HEAD_DIM + 1, :])

    kmax = jnp.sqrt(jnp.max(ksq_ref[0], axis=1, keepdims=True))
    for h in range(N_HEADS):
        q = qt_ref[0, h].astype(F32)
        qn = jnp.sqrt(jnp.sum(q * q, axis=0, keepdims=True))
        stab_ref[h] = qn * kmax[h // GROUP:h // GROUP + 1]
    min_denom = sweep()

    @pl.when(min_denom < MIN_DENOM)
    def _():
        stab_ref[...] = jnp.full(stab_ref.shape, -jnp.inf, F32)

        def body(j, carry):
            kc = k_chunk(j)
            for h in range(N_HEADS):
                st = jnp.dot(kc, qt_ref[0, h], preferred_element_type=F32)
                stab_ref[h] = jnp.maximum(stab_ref[h], jnp.max(st, axis=0, keepdims=True))
            return carry

        lax.fori_loop(0, nkv, body, 0)
        sweep()

    outs = []
    for h in range(N_HEADS):
        a = acc_ref[h]
        outs.append(a[:HEAD_DIM] / a[HEAD_DIM:HEAD_DIM + 1])
    o = jnp.concatenate(outs, axis=0).T
    o_ref[0] = _rms(o, ag_ref[...]).astype(BF)


def _attn_call(qt, k, ksq, vt, ag):
    B, _, _, S = qt.shape
    nkv, tk = vt.shape[2], vt.shape[4]
    tq = min(Q_TILE, S)
    return pl.pallas_call(
        _attn_kernel,
        grid=(B, S // tq),
        in_specs=[
            pl.BlockSpec((1, N_HEADS, KV_W, tq), lambda b, i: (b, 0, 0, i)),
            pl.BlockSpec((1, S, KV_W), lambda b, i: (b, 0, 0)),
            pl.BlockSpec((1, N_KV_HEADS, S), lambda b, i: (b, 0, 0)),
            pl.BlockSpec((1, N_KV_HEADS, nkv, VT_ROWS, tk), lambda b, i: (b, 0, 0, 0, 0)),
            _const_spec((1, ATTN_W)),
        ],
        out_specs=pl.BlockSpec((1, tq, ATTN_W), lambda b, i: (b, i, 0)),
        out_shape=jax.ShapeDtypeStruct((B, S, ATTN_W), BF),
        scratch_shapes=[
            pltpu.VMEM((N_HEADS, 1, tq), F32),
            pltpu.VMEM((N_HEADS, VT_ROWS, tq), F32),
        ],
        compiler_params=pltpu.CompilerParams(
            dimension_semantics=("parallel", "arbitrary"), vmem_limit_bytes=VMEM_LIMIT),
        name="attn",
    )(qt, k, ksq, vt, ag)


def _mix_ffn_kernel(x_ref, an_ref, gn_ref, wo_ref, pmg_ref, pfg_ref, w1_ref, w2_ref, pog_ref, o_ref):
    cat = jnp.concatenate([an_ref[...], gn_ref[...]], axis=-1)
    m = jnp.dot(cat, wo_ref[...], preferred_element_type=F32)
    x1 = x_ref[...] + _rms(m, pmg_ref[...])
    h = _rms(x1, pfg_ref[...]).astype(BF)
    f = None
    for c in range(D_FF // FF_CHUNK):
        a = jnp.dot(h, w1_ref[:, c * FF_CHUNK:(c + 1) * FF_CHUNK], preferred_element_type=F32)
        a = jnp.maximum(a, 0.0)
        a = (a * a).astype(BF)
        part = jnp.dot(a, w2_ref[c * FF_CHUNK:(c + 1) * FF_CHUNK, :], preferred_element_type=F32)
        f = part if f is None else f + part
    o_ref[...] = x1 + _rms(f, pog_ref[...])


def _mix_ffn_call(x, an, gn, wo, pmg, pfg, w1, w2, pog):
    n_rows = x.shape[0]
    tm = min(ROW_TILE, n_rows)
    row = lambda width: pl.BlockSpec((tm, width), lambda i: (i, 0))
    return pl.pallas_call(
        _mix_ffn_kernel,
        grid=(n_rows // tm,),
        in_specs=[
            row(D_MODEL), row(ATTN_W), row(SG_W),
            _const_spec((D_MODEL, D_MODEL)),
            _const_spec((1, D_MODEL)),
            _const_spec((1, D_MODEL)),
            _const_spec((D_MODEL, D_FF)),
            _const_spec((D_FF, D_MODEL)),
            _const_spec((1, D_MODEL)),
        ],
        out_specs=row(D_MODEL),
        out_shape=jax.ShapeDtypeStruct((n_rows, D_MODEL), F32),
        compiler_params=pltpu.CompilerParams(
            dimension_semantics=("parallel",), vmem_limit_bytes=VMEM_LIMIT),
        name="mix_ffn",
    )(x, an, gn, wo, pmg, pfg, w1, w2, pog)


def _rope_tables_t(seq_len):
    pos = jnp.arange(seq_len)
    r = (pos // GRID_W).astype(F32)
    c = (pos % GRID_W).astype(F32)
    inv = ROPE_THETA ** (-jnp.arange(0, AXIS_DIM, 2, dtype=F32) / AXIS_DIM)
    ang_r = inv[:, None] * r[None, :]
    ang_c = inv[:, None] * c[None, :]
    ang = jnp.concatenate([ang_r, ang_r, ang_c, ang_c], axis=0)
    half = AXIS_DIM // 2
    sign = jnp.concatenate([-jnp.ones((half, 1), F32), jnp.ones((half, 1), F32)] * 2, axis=0)
    return jnp.cos(ang), jnp.sin(ang) * sign


def _trunk(x, p):
    B, S, _ = x.shape
    cos_t, sin_t = _rope_tables_t(S)
    depth = p["w_in"].shape[0]
    for l in range(depth):
        qt, k, ksq, vt, gn = _proj_call(
            x, p["pre_mix_g"][l][None, :], p["w_in"][l], p["q_norm_g"][l][:, None], p["k_norm_g"][l][:, None],
            cos_t, sin_t, p["sg_norm_g"][l][None, :], p["sg_norm_b"][l][None, :], p["sg_w"][l], p["sg_b"][l],
            p["sg_out_g"][l][None, :])
        an = _attn_call(qt, k, ksq, vt, p["attn_out_g"][l][None, :])
        x = _mix_ffn_call(
            x.reshape(B * S, D_MODEL), an.reshape(B * S, ATTN_W), gn.reshape(B * S, SG_W), p["w_out"][l],
            p["post_mix_g"][l][None, :], p["pre_ffn_g"][l][None, :], p["w_ff1"][l], p["w_ff2"][l],
            p["post_ffn_g"][l][None, :]).reshape(B, S, D_MODEL)
    return x


def kernel(x_prompt, x_sample, w_in, w_out, q_norm_g, k_norm_g, sg_norm_g, sg_norm_b, sg_w, sg_b, attn_out_g,
           sg_out_g, pre_mix_g, post_mix_g, pre_ffn_g, post_ffn_g, w_ff1, w_ff2):
    depth = w_in.shape[0]
    p = dict(
        w_in=w_in.astype(BF), w_out=w_out.astype(BF), w_ff1=w_ff1.astype(BF), w_ff2=w_ff2.astype(BF),
        q_norm_g=q_norm_g, k_norm_g=k_norm_g, sg_norm_g=sg_norm_g, sg_norm_b=sg_norm_b,
        sg_w=sg_w.astype(BF).reshape(depth, SG_PAIRS, 2 * CHUNK, CHUNK),
        sg_b=jnp.repeat(jnp.swapaxes(sg_b, 1, 2), SG_W // SG_GROUPS, axis=2),
        attn_out_g=attn_out_g, sg_out_g=sg_out_g, pre_mix_g=pre_mix_g, post_mix_g=post_mix_g,
        pre_ffn_g=pre_ffn_g, post_ffn_g=post_ffn_g,
    )
    return (_trunk(x_prompt, p), _trunk(x_sample, p))
```

```python
import functools

import jax
import jax.numpy as jnp
from jax import lax
from jax.experimental import pallas as pl
from jax.experimental.pallas import tpu as pltpu

D_MODEL = 1024
HEAD_DIM = 64
N_HEADS = 8
N_KV_HEADS = 2
GROUP = N_HEADS // N_KV_HEADS
ATTN_W = N_HEADS * HEAD_DIM
KV_W = N_KV_HEADS * HEAD_DIM
QKV_W = ATTN_W + 2 * KV_W
SG_W = 512
SG_GROUPS = 8
SG_PAIRS = SG_GROUPS // 2
CHUNK = 128
GRID_W = 64
AXIS_DIM = HEAD_DIM // 2
ROPE_THETA = 10000.0
D_FF = 4 * D_MODEL
D_IN = QKV_W + 2 * SG_W
EPS = 1e-6

LANES = 128
BF16_SUBLANES = 16
VT_ROWS = HEAD_DIM + BF16_SUBLANES

ROW_TILE = 512
Q_TILE = 256
FF_CHUNK = 1024
S_AHEAD = 4
ITEM_KEYS = 256
KV_UNROLL = 4
LOG2_E = 1.4426950408889634
MIN_DENOM = 2.0 ** -60
VMEM_LIMIT = 56 * 1024 * 1024

BF = jnp.bfloat16
F32 = jnp.float32


def _rms(x, g):
    ms = jnp.mean(x * x, axis=-1, keepdims=True)
    return x * lax.rsqrt(ms + EPS) * g


def _gelu_tanh(x):
    c = 0.7978845608028654
    return 0.5 * x * (1.0 + jnp.tanh(c * (x + 0.044715 * (x * x * x))))


def _const_spec(shape):
    nd = len(shape)
    return pl.BlockSpec(shape, lambda *_: (0,) * nd, pipeline_mode=pl.Buffered(1))


def _proj_kernel(x_ref, g_ref, w_ref, qg_ref, kg_ref, cos_ref, sin_ref, lng_ref, lnb_ref, ws_ref, bs_ref,
                 sgo_ref, qt_ref, k_ref, ksq_ref, vt_ref, gn_ref):
    x = x_ref[0]
    tm = x.shape[0]
    h = _rms(x, g_ref[...]).astype(BF)
    z = jnp.dot(h, w_ref[...], preferred_element_type=F32)

    zt = z[:, :QKV_W].T
    cos = cos_ref[...]
    sin = sin_ref[...]

    def norm_rope(blk, gcol):
        ms = jnp.mean(blk * blk, axis=0, keepdims=True)
        y = blk * lax.rsqrt(ms + EPS) * gcol
        half = AXIS_DIM // 2
        sw = jnp.concatenate([y[half:2 * half], y[0:half], y[3 * half:4 * half], y[2 * half:3 * half]], axis=0)
        return y * cos + sw * sin

    zero_blk = jnp.zeros((HEAD_DIM, tm), BF)
    for hh in range(N_HEADS):
        qh = norm_rope(zt[hh * HEAD_DIM:(hh + 1) * HEAD_DIM], qg_ref[...]) * (HEAD_DIM ** -0.5 * LOG2_E)
        c = hh // GROUP
        for cc in range(N_KV_HEADS):
            qt_ref[0, hh, cc * HEAD_DIM:(cc + 1) * HEAD_DIM, :] = qh.astype(BF) if cc == c else zero_blk
    kt = jnp.concatenate(
        [norm_rope(zt[ATTN_W + c * HEAD_DIM:ATTN_W + (c + 1) * HEAD_DIM], kg_ref[...]) for c in range(N_KV_HEADS)],
        axis=0)
    k_ref[0] = kt.T.astype(BF)
    kf = kt.astype(BF).astype(F32)
    kf = kf * kf
    ksq_ref[0] = jnp.concatenate(
        [jnp.sum(kf[c * HEAD_DIM:(c + 1) * HEAD_DIM], axis=0, keepdims=True) for c in range(N_KV_HEADS)], axis=0)
    ones_blk = jnp.ones((BF16_SUBLANES, tm), BF)
    for c in range(N_KV_HEADS):
        v0 = ATTN_W + KV_W + c * HEAD_DIM
        vt_ref[0, c, 0, 0:HEAD_DIM, :] = zt[v0:v0 + HEAD_DIM].astype(BF)
        vt_ref[0, c, 0, HEAD_DIM:VT_ROWS, :] = ones_blk

    gz = _gelu_tanh(z[:, QKV_W:])
    u = gz[:, :SG_W]
    vv = gz[:, SG_W:]
    mu = jnp.mean(vv, axis=-1, keepdims=True)
    xc = vv - mu
    var = jnp.mean(xc * xc, axis=-1, keepdims=True)
    vb = (xc * lax.rsqrt(var + EPS) * lng_ref[...] + lnb_ref[...]).astype(BF)
    nch = tm // CHUNK
    low_half = lax.broadcasted_iota(jnp.int32, (CHUNK, LANES), 1) < SG_W // SG_GROUPS
    pair_cols = []
    for j in range(SG_PAIRS):
        rhs = jnp.concatenate(
            [vb[c * CHUNK:(c + 1) * CHUNK, j * LANES:(j + 1) * LANES] for c in range(nch)], axis=1)
        r = jnp.dot(ws_ref[j], rhs, preferred_element_type=F32)
        pair_cols.append(jnp.concatenate(
            [jnp.where(low_half, r[:CHUNK, c * LANES:(c + 1) * LANES], r[CHUNK:, c * LANES:(c + 1) * LANES])
             for c in range(nch)], axis=0))
    bias = jnp.concatenate([bs_ref[...]] * nch, axis=0)
    mixed = jnp.concatenate(pair_cols, axis=1) + bias
    gn_ref[0] = _rms(u * mixed, sgo_ref[...]).astype(BF)


def _proj_call(x, g, w, qg, kg, cos_t, sin_t, lng, lnb, ws, bs, sgo):
    B, S, _ = x.shape
    tm = min(ROW_TILE, S)
    n = S // tm
    out_shape = (
        jax.ShapeDtypeStruct((B, N_HEADS, KV_W, S), BF),
        jax.ShapeDtypeStruct((B, S, KV_W), BF),
        jax.ShapeDtypeStruct((B, N_KV_HEADS, S), F32),
        jax.ShapeDtypeStruct((B, N_KV_HEADS, n, VT_ROWS, tm), BF),
        jax.ShapeDtypeStruct((B, S, SG_W), BF),
    )
    return pl.pallas_call(
        _proj_kernel,
        grid=(B, n),
        in_specs=[
            pl.BlockSpec((1, tm, D_MODEL), lambda b, i: (b, i, 0)),
            _const_spec((1, D_MODEL)),
            _const_spec((D_MODEL, D_IN)),
            _const_spec((HEAD_DIM, 1)),
            _const_spec((HEAD_DIM, 1)),
            pl.BlockSpec((HEAD_DIM, tm), lambda b, i: (0, i)),
            pl.BlockSpec((HEAD_DIM, tm), lambda b, i: (0, i)),
            _const_spec((1, SG_W)),
            _const_spec((1, SG_W)),
            _const_spec((SG_PAIRS, 2 * CHUNK, CHUNK)),
            _const_spec((CHUNK, SG_W)),
            _const_spec((1, SG_W)),
        ],
        out_specs=(
            pl.BlockSpec((1, N_HEADS, KV_W, tm), lambda b, i: (b, 0, 0, i)),
            pl.BlockSpec((1, tm, KV_W), lambda b, i: (b, i, 0)),
            pl.BlockSpec((1, N_KV_HEADS, tm), lambda b, i: (b, 0, i)),
            pl.BlockSpec((1, N_KV_HEADS, 1, VT_ROWS, tm), lambda b, i: (b, 0, i, 0, 0)),
            pl.BlockSpec((1, tm, SG_W), lambda b, i: (b, i, 0)),
        ),
        out_shape=out_shape,
        compiler_params=pltpu.CompilerParams(
            dimension_semantics=("parallel", "parallel"), vmem_limit_bytes=VMEM_LIMIT),
        name="proj",
    )(x, g, w, qg, kg, cos_t, sin_t, lng, lnb, ws, bs, sgo)


def _attn_kernel(qt_ref, k_ref, ksq_ref, vt_ref, ag_ref, o_ref, stab_ref, acc_ref):
    nkv = vt_ref.shape[2]
    tk = vt_ref.shape[4]
    unroll = KV_UNROLL if nkv % KV_UNROLL == 0 else 1

    ik = min(ITEM_KEYS, tk)

    def k_chunk(j):
        return k_ref[0, pl.ds(pl.multiple_of(j * tk, tk), tk), :]

    def k_item(j, s):
        return k_ref[0, pl.ds(pl.multiple_of(j * tk + s * ik, ik), ik), :]

    def sweep():
        acc_ref[...] = jnp.zeros(acc_ref.shape, F32)

        def body(jj, carry):
            items = [(jj * unroll + c, s, h)
                     for c in range(unroll) for s in range(tk // ik) for h in range(N_HEADS)]
            score = lambda j, s, h: jnp.dot(k_item(j, s), qt_ref[0, h], preferred_element_type=F32)
            ahead = [score(*it) for it in items[:S_AHEAD]]
            for n, (j, s, h) in enumerate(items):
                if n + S_AHEAD < len(items):
                    ahead.append(score(*items[n + S_AHEAD]))
                st = ahead.pop(0)
                p = jnp.exp2(st - stab_ref[h]).astype(BF)
                acc_ref[h] += jnp.dot(vt_ref[0, h // GROUP, j, :, s * ik:(s + 1) * ik], p,
                                      preferred_element_type=F32)
            return carry

        lax.fori_loop(0, nkv // unroll, body, 0)
        return jnp.min(acc_ref[:, HEAD_DIM:HEAD_DIM + 1, :])

    kmax = jnp.sqrt(jnp.max(ksq_ref[0], axis=1, keepdims=True))
    for h in range(N_HEADS):
        q = qt_ref[0, h].astype(F32)
        qn = jnp.sqrt(jnp.sum(q * q, axis=0, keepdims=True))
        stab_ref[h] = qn * kmax[h // GROUP:h // GROUP + 1]
    min_denom = sweep()

    @pl.when(min_denom < MIN_DENOM)
    def _():
        stab_ref[...] = jnp.full(stab_ref.shape, -jnp.inf, F32)

        def body(j, carry):
            kc = k_chunk(j)
            for h in range(N_HEADS):
                st = jnp.dot(kc, qt_ref[0, h], preferred_element_type=F32)
                stab_ref[h] = jnp.maximum(stab_ref[h], jnp.max(st, axis=0, keepdims=True))
            return carry

        lax.fori_loop(0, nkv, body, 0)
        sweep()

    outs = []
    for h in range(N_HEADS):
        a = acc_ref[h]
        outs.append(a[:HEAD_DIM] / a[HEAD_DIM:HEAD_DIM + 1])
    o = jnp.concatenate(outs, axis=0).T
    o_ref[0] = _rms(o, ag_ref[...]).astype(BF)


def _attn_call(qt, k, ksq, vt, ag):
    B, _, _, S = qt.shape
    nkv, tk = vt.shape[2], vt.shape[4]
    tq = min(Q_TILE, S)
    return pl.pallas_call(
        _attn_kernel,
        grid=(B, S // tq),
        in_specs=[
            pl.BlockSpec((1, N_HEADS, KV_W, tq), lambda b, i: (b, 0, 0, i)),
            pl.BlockSpec((1, S, KV_W), lambda b, i: (b, 0, 0)),
            pl.BlockSpec((1, N_KV_HEADS, S), lambda b, i: (b, 0, 0)),
            pl.BlockSpec((1, N_KV_HEADS, nkv, VT_ROWS, tk), lambda b, i: (b, 0, 0, 0, 0)),
            _const_spec((1, ATTN_W)),
        ],
        out_specs=pl.BlockSpec((1, tq, ATTN_W), lambda b, i: (b, i, 0)),
        out_shape=jax.ShapeDtypeStruct((B, S, ATTN_W), BF),
        scratch_shapes=[
            pltpu.VMEM((N_HEADS, 1, tq), F32),
            pltpu.VMEM((N_HEADS, VT_ROWS, tq), F32),
        ],
        compiler_params=pltpu.CompilerParams(
            dimension_semantics=("parallel", "arbitrary"), vmem_limit_bytes=VMEM_LIMIT),
        name="attn",
    )(qt, k, ksq, vt, ag)


def _mix_ffn_kernel(x_ref, an_ref, gn_ref, wo_ref, pmg_ref, pfg_ref, w1_ref, w2_ref, pog_ref, o_ref):
    cat = jnp.concatenate([an_ref[...], gn_ref[...]], axis=-1)
    m = jnp.dot(cat, wo_ref[...], preferred_element_type=F32)
    x1 = x_ref[...] + _rms(m, pmg_ref[...])
    h = _rms(x1, pfg_ref[...]).astype(BF)
    f = None
    for c in range(D_FF // FF_CHUNK):
        a = jnp.dot(h, w1_ref[:, c * FF_CHUNK:(c + 1) * FF_CHUNK], preferred_element_type=F32)
        a = jnp.maximum(a, 0.0)
        a = (a * a).astype(BF)
        part = jnp.dot(a, w2_ref[c * FF_CHUNK:(c + 1) * FF_CHUNK, :], preferred_element_type=F32)
        f = part if f is None else f + part
    o_ref[...] = x1 + _rms(f, pog_ref[...])


def _mix_ffn_call(x, an, gn, wo, pmg, pfg, w1, w2, pog):
    n_rows = x.shape[0]
    tm = min(ROW_TILE, n_rows)
    row = lambda width: pl.BlockSpec((tm, width), lambda i: (i, 0))
    return pl.pallas_call(
        _mix_ffn_kernel,
        grid=(n_rows // tm,),
        in_specs=[
            row(D_MODEL), row(ATTN_W), row(SG_W),
            _const_spec((D_MODEL, D_MODEL)),
            _const_spec((1, D_MODEL)),
            _const_spec((1, D_MODEL)),
            _const_spec((D_MODEL, D_FF)),
            _const_spec((D_FF, D_MODEL)),
            _const_spec((1, D_MODEL)),
        ],
        out_specs=row(D_MODEL),
        out_shape=jax.ShapeDtypeStruct((n_rows, D_MODEL), F32),
        compiler_params=pltpu.CompilerParams(
            dimension_semantics=("parallel",), vmem_limit_bytes=VMEM_LIMIT),
        name="mix_ffn",
    )(x, an, gn, wo, pmg, pfg, w1, w2, pog)


def _rope_tables_t(seq_len):
    pos = jnp.arange(seq_len)
    r = (pos // GRID_W).astype(F32)
    c = (pos % GRID_W).astype(F32)
    inv = ROPE_THETA ** (-jnp.arange(0, AXIS_DIM, 2, dtype=F32) / AXIS_DIM)
    ang_r = inv[:, None] * r[None, :]
    ang_c = inv[:, None] * c[None, :]
    ang = jnp.concatenate([ang_r, ang_r, ang_c, ang_c], axis=0)
    half = AXIS_DIM // 2
    sign = jnp.concatenate([-jnp.ones((half, 1), F32), jnp.ones((half, 1), F32)] * 2, axis=0)
    return jnp.cos(ang), jnp.sin(ang) * sign


def _trunk(x, p):
    B, S, _ = x.shape
    cos_t, sin_t = _rope_tables_t(S)
    depth = p["w_in"].shape[0]
    for l in range(depth):
        qt, k, ksq, vt, gn = _proj_call(
            x, p["pre_mix_g"][l][None, :], p["w_in"][l], p["q_norm_g"][l][:, None], p["k_norm_g"][l][:, None],
            cos_t, sin_t, p["sg_norm_g"][l][None, :], p["sg_norm_b"][l][None, :], p["sg_w"][l], p["sg_b"][l],
            p["sg_out_g"][l][None, :])
        an = _attn_call(qt, k, ksq, vt, p["attn_out_g"][l][None, :])
        x = _mix_ffn_call(
            x.reshape(B * S, D_MODEL), an.reshape(B * S, ATTN_W), gn.reshape(B * S, SG_W), p["w_out"][l],
            p["post_mix_g"][l][None, :], p["pre_ffn_g"][l][None, :], p["w_ff1"][l], p["w_ff2"][l],
            p["post_ffn_g"][l][None, :]).reshape(B, S, D_MODEL)
    return x


def kernel(x_prompt, x_sample, w_in, w_out, q_norm_g, k_norm_g, sg_norm_g, sg_norm_b, sg_w, sg_b, attn_out_g,
           sg_out_g, pre_mix_g, post_mix_g, pre_ffn_g, post_ffn_g, w_ff1, w_ff2):
    depth = w_in.shape[0]
    p = dict(
        w_in=w_in.astype(BF), w_out=w_out.astype(BF), w_ff1=w_ff1.astype(BF), w_ff2=w_ff2.astype(BF),
        q_norm_g=q_norm_g, k_norm_g=k_norm_g, sg_norm_g=sg_norm_g, sg_norm_b=sg_norm_b,
        sg_w=sg_w.astype(BF).reshape(depth, SG_PAIRS, 2 * CHUNK, CHUNK),
        sg_b=jnp.repeat(jnp.swapaxes(sg_b, 1, 2), SG_W // SG_GROUPS, axis=2),
        attn_out_g=attn_out_g, sg_out_g=sg_out_g, pre_mix_g=pre_mix_g, post_mix_g=post_mix_g,
        pre_ffn_g=pre_ffn_g, post_ffn_g=post_ffn_g,
    )
    return (_trunk(x_prompt, p), _trunk(x_sample, p))
```

```python
import functools

import jax
import jax.numpy as jnp
from jax import lax
from jax.experimental import pallas as pl
from jax.experimental.pallas import tpu as pltpu

D_MODEL = 1024
HEAD_DIM = 64
N_HEADS = 8
N_KV_HEADS = 2
GROUP = N_HEADS // N_KV_HEADS
ATTN_W = N_HEADS * HEAD_DIM
KV_W = N_KV_HEADS * HEAD_DIM
QKV_W = ATTN_W + 2 * KV_W
SG_W = 512
SG_GROUPS = 8
SG_PAIRS = SG_GROUPS // 2
CHUNK = 128
GRID_W = 64
AXIS_DIM = HEAD_DIM // 2
ROPE_THETA = 10000.0
D_FF = 4 * D_MODEL
D_IN = QKV_W + 2 * SG_W
EPS = 1e-6

LANES = 128
SUBLANES = 8

ROW_TILE = 512
Q_TILE = 256
FF_CHUNK = 1024
PROJ_SUB = 128
PROJ_AHEAD = 4
S_AHEAD = 4
ITEM_KEYS = 256
KV_UNROLL = 8
LOG2_E = 1.4426950408889634
MIN_DENOM = 2.0 ** -60
VMEM_LIMIT = 56 * 1024 * 1024

BF = jnp.bfloat16
F32 = jnp.float32


def _rms(x, g):
    ms = jnp.mean(x * x, axis=-1, keepdims=True)
    return x * lax.rsqrt(ms + EPS) * g


def _gelu_tanh(x):
    c = 0.7978845608028654
    return 0.5 * x * (1.0 + jnp.tanh(c * (x + 0.044715 * (x * x * x))))


def _const_spec(shape):
    nd = len(shape)
    return pl.BlockSpec(shape, lambda *_: (0,) * nd, pipeline_mode=pl.Buffered(1))


def _proj_kernel(x_ref, g_ref, w_ref, qg_ref, kg_ref, cos_ref, sin_ref, lng_ref, lnb_ref, ws_ref, bs_ref,
                 sgo_ref, qt_ref, k_ref, ksq_ref, vt_ref, gn_ref):
    tm = x_ref.shape[1]
    sub = min(PROJ_SUB, tm)
    nsub = tm // sub

    def project(i):
        h = _rms(x_ref[0, i * sub:(i + 1) * sub], g_ref[...]).astype(BF)
        return jnp.dot(h, w_ref[...], preferred_element_type=F32)

    ahead = [project(i) for i in range(min(PROJ_AHEAD, nsub))]
    for i in range(nsub):
        if i + PROJ_AHEAD < nsub:
            ahead.append(project(i + PROJ_AHEAD))
        _proj_tail(ahead.pop(0), i * sub, sub, qg_ref, kg_ref, cos_ref, sin_ref, lng_ref, lnb_ref, ws_ref, bs_ref,
                   sgo_ref, qt_ref, k_ref, ksq_ref, vt_ref, gn_ref)


def _proj_tail(z, r0, sub, qg_ref, kg_ref, cos_ref, sin_ref, lng_ref, lnb_ref, ws_ref, bs_ref, sgo_ref,
               qt_ref, k_ref, ksq_ref, vt_ref, gn_ref):
    rows = slice(r0, r0 + sub)
    zt = z[:, :QKV_W].T
    cos = cos_ref[:, rows]
    sin = sin_ref[:, rows]

    def norm_rope(blk, gcol):
        ms = jnp.mean(blk * blk, axis=0, keepdims=True)
        y = blk * lax.rsqrt(ms + EPS) * gcol
        half = AXIS_DIM // 2
        sw = jnp.concatenate([y[half:2 * half], y[0:half], y[3 * half:4 * half], y[2 * half:3 * half]], axis=0)
        return y * cos + sw * sin

    zero_blk = jnp.zeros((HEAD_DIM, sub), BF)
    for hh in range(N_HEADS):
        qh = norm_rope(zt[hh * HEAD_DIM:(hh + 1) * HEAD_DIM], qg_ref[...]) * (HEAD_DIM ** -0.5 * LOG2_E)
        c = hh // GROUP
        for cc in range(N_KV_HEADS):
            qt_ref[0, hh, cc * HEAD_DIM:(cc + 1) * HEAD_DIM, rows] = qh.astype(BF) if cc == c else zero_blk
    kt = jnp.concatenate(
        [norm_rope(zt[ATTN_W + c * HEAD_DIM:ATTN_W + (c + 1) * HEAD_DIM], kg_ref[...]) for c in range(N_KV_HEADS)],
        axis=0)
    k_ref[0, rows] = kt.T.astype(BF)
    kf = kt.astype(BF).astype(F32)
    kf = kf * kf
    ksq_ref[0, :, rows] = jnp.concatenate(
        [jnp.sum(kf[c * HEAD_DIM:(c + 1) * HEAD_DIM], axis=0, keepdims=True) for c in range(N_KV_HEADS)], axis=0)
    for c in range(N_KV_HEADS):
        v0 = ATTN_W + KV_W + c * HEAD_DIM
        vt_ref[0, c, 0, :, rows] = zt[v0:v0 + HEAD_DIM].astype(BF)

    gz = _gelu_tanh(z[:, QKV_W:])
    u = gz[:, :SG_W]
    vv = gz[:, SG_W:]
    mu = jnp.mean(vv, axis=-1, keepdims=True)
    xc = vv - mu
    var = jnp.mean(xc * xc, axis=-1, keepdims=True)
    vb = (xc * lax.rsqrt(var + EPS) * lng_ref[...] + lnb_ref[...]).astype(BF)
    nch = sub // CHUNK
    low_half = lax.broadcasted_iota(jnp.int32, (CHUNK, LANES), 1) < SG_W // SG_GROUPS
    pair_cols = []
    for j in range(SG_PAIRS):
        rhs = jnp.concatenate(
            [vb[c * CHUNK:(c + 1) * CHUNK, j * LANES:(j + 1) * LANES] for c in range(nch)], axis=1)
        r = jnp.dot(ws_ref[j], rhs, preferred_element_type=F32)
        pair_cols.append(jnp.concatenate(
            [jnp.where(low_half, r[:CHUNK, c * LANES:(c + 1) * LANES], r[CHUNK:, c * LANES:(c + 1) * LANES])
             for c in range(nch)], axis=0))
    bias = jnp.concatenate([bs_ref[...]] * nch, axis=0)
    mixed = jnp.concatenate(pair_cols, axis=1) + bias
    gn_ref[0, rows] = _rms(u * mixed, sgo_ref[...]).astype(BF)


def _proj_call(x, g, w, qg, kg, cos_t, sin_t, lng, lnb, ws, bs, sgo):
    B, S, _ = x.shape
    tm = min(ROW_TILE, S)
    n = S // tm
    out_shape = (
        jax.ShapeDtypeStruct((B, N_HEADS, KV_W, S), BF),
        jax.ShapeDtypeStruct((B, S, KV_W), BF),
        jax.ShapeDtypeStruct((B, N_KV_HEADS, S), F32),
        jax.ShapeDtypeStruct((B, N_KV_HEADS, n, HEAD_DIM, tm), BF),
        jax.ShapeDtypeStruct((B, S, SG_W), BF),
    )
    return pl.pallas_call(
        _proj_kernel,
        grid=(B, n),
        in_specs=[
            pl.BlockSpec((1, tm, D_MODEL), lambda b, i: (b, i, 0)),
            _const_spec((1, D_MODEL)),
            _const_spec((D_MODEL, D_IN)),
            _const_spec((HEAD_DIM, 1)),
            _const_spec((HEAD_DIM, 1)),
            pl.BlockSpec((HEAD_DIM, tm), lambda b, i: (0, i)),
            pl.BlockSpec((HEAD_DIM, tm), lambda b, i: (0, i)),
            _const_spec((1, SG_W)),
            _const_spec((1, SG_W)),
            _const_spec((SG_PAIRS, 2 * CHUNK, CHUNK)),
            _const_spec((CHUNK, SG_W)),
            _const_spec((1, SG_W)),
        ],
        out_specs=(
            pl.BlockSpec((1, N_HEADS, KV_W, tm), lambda b, i: (b, 0, 0, i)),
            pl.BlockSpec((1, tm, KV_W), lambda b, i: (b, i, 0)),
            pl.BlockSpec((1, N_KV_HEADS, tm), lambda b, i: (b, 0, i)),
            pl.BlockSpec((1, N_KV_HEADS, 1, HEAD_DIM, tm), lambda b, i: (b, 0, i, 0, 0)),
            pl.BlockSpec((1, tm, SG_W), lambda b, i: (b, i, 0)),
        ),
        out_shape=out_shape,
        compiler_params=pltpu.CompilerParams(
            dimension_semantics=("parallel", "parallel"), vmem_limit_bytes=VMEM_LIMIT),
        name="proj",
    )(x, g, w, qg, kg, cos_t, sin_t, lng, lnb, ws, bs, sgo)


def _attn_kernel(qt_ref, k_ref, ksq_ref, vt_ref, ag_ref, o_ref, stab_ref, acc_ref, den_ref):
    nkv = vt_ref.shape[2]
    tk = vt_ref.shape[4]
    unroll = KV_UNROLL if nkv % KV_UNROLL == 0 else 1

    ik = min(ITEM_KEYS, tk)

    def k_chunk(j):
        return k_ref[0, pl.ds(pl.multiple_of(j * tk, tk), tk), :]

    def k_item(j, s):
        return k_ref[0, pl.ds(pl.multiple_of(j * tk + s * ik, ik), ik), :]

    def sweep():
        acc_ref[...] = jnp.zeros(acc_ref.shape, F32)
        den_ref[...] = jnp.zeros(den_ref.shape, F32)

        def body(jj, carry):
            items = [(jj * unroll + c, s, h)
                     for c in range(unroll) for s in range(tk // ik) for h in range(N_HEADS)]
            score = lambda j, s, h: jnp.dot(k_item(j, s), qt_ref[0, h], preferred_element_type=F32)
            ahead = [score(*it) for it in items[:S_AHEAD]]
            for n, (j, s, h) in enumerate(items):
                if n + S_AHEAD < len(items):
                    ahead.append(score(*items[n + S_AHEAD]))
                st = ahead.pop(0)
                p = jnp.exp2(st - stab_ref[h])
                den_ref[h] += jnp.sum(p.reshape(ik // SUBLANES, SUBLANES, p.shape[1]), axis=0)
                acc_ref[h] += jnp.dot(vt_ref[0, h // GROUP, j, :, s * ik:(s + 1) * ik], p.astype(BF),
                                      preferred_element_type=F32)
            return carry

        lax.fori_loop(0, nkv // unroll, body, 0)
        return jnp.min(jnp.sum(den_ref[...], axis=1))

    kmax = jnp.sqrt(jnp.max(ksq_ref[0], axis=1, keepdims=True))
    for h in range(N_HEADS):
        q = qt_ref[0, h].astype(F32)
        qn = jnp.sqrt(jnp.sum(q * q, axis=0, keepdims=True))
        stab_ref[h] = qn * kmax[h // GROUP:h // GROUP + 1]
    min_denom = sweep()

    @pl.when(min_denom < MIN_DENOM)
    def _():
        stab_ref[...] = jnp.full(stab_ref.shape, -jnp.inf, F32)

        def body(j, carry):
            kc = k_chunk(j)
            for h in range(N_HEADS):
                st = jnp.dot(kc, qt_ref[0, h], preferred_element_type=F32)
                stab_ref[h] = jnp.maximum(stab_ref[h], jnp.max(st, axis=0, keepdims=True))
            return carry

        lax.fori_loop(0, nkv, body, 0)
        sweep()

    outs = []
    for h in range(N_HEADS):
        outs.append(acc_ref[h] / jnp.sum(den_ref[h], axis=0, keepdims=True))
    o = jnp.concatenate(outs, axis=0).T
    o_ref[0] = _rms(o, ag_ref[...]).astype(BF)


def _attn_call(qt, k, ksq, vt, ag):
    B, _, _, S = qt.shape
    nkv, tk = vt.shape[2], vt.shape[4]
    tq = min(Q_TILE, S)
    return pl.pallas_call(
        _attn_kernel,
        grid=(B, S // tq),
        in_specs=[
            pl.BlockSpec((1, N_HEADS, KV_W, tq), lambda b, i: (b, 0, 0, i)),
            pl.BlockSpec((1, S, KV_W), lambda b, i: (b, 0, 0)),
            pl.BlockSpec((1, N_KV_HEADS, S), lambda b, i: (b, 0, 0)),
            pl.BlockSpec((1, N_KV_HEADS, nkv, HEAD_DIM, tk), lambda b, i: (b, 0, 0, 0, 0)),
            _const_spec((1, ATTN_W)),
        ],
        out_specs=pl.BlockSpec((1, tq, ATTN_W), lambda b, i: (b, i, 0)),
        out_shape=jax.ShapeDtypeStruct((B, S, ATTN_W), BF),
        scratch_shapes=[
            pltpu.VMEM((N_HEADS, 1, tq), F32),
            pltpu.VMEM((N_HEADS, HEAD_DIM, tq), F32),
            pltpu.VMEM((N_HEADS, SUBLANES, tq), F32),
        ],
        compiler_params=pltpu.CompilerParams(
            dimension_semantics=("parallel", "arbitrary"), vmem_limit_bytes=VMEM_LIMIT),
        name="attn",
    )(qt, k, ksq, vt, ag)


def _mix_ffn_kernel(x_ref, an_ref, gn_ref, wo_ref, pmg_ref, pfg_ref, w1_ref, w2_ref, pog_ref, o_ref):
    cat = jnp.concatenate([an_ref[...], gn_ref[...]], axis=-1)
    m = jnp.dot(cat, wo_ref[...], preferred_element_type=F32)
    x1 = x_ref[...] + _rms(m, pmg_ref[...])
    h = _rms(x1, pfg_ref[...]).astype(BF)
    f = None
    for c in range(D_FF // FF_CHUNK):
        a = jnp.dot(h, w1_ref[:, c * FF_CHUNK:(c + 1) * FF_CHUNK], preferred_element_type=F32)
        a = jnp.maximum(a, 0.0)
        a = (a * a).astype(BF)
        part = jnp.dot(a, w2_ref[c * FF_CHUNK:(c + 1) * FF_CHUNK, :], preferred_element_type=F32)
        f = part if f is None else f + part
    o_ref[...] = x1 + _rms(f, pog_ref[...])


def _mix_ffn_call(x, an, gn, wo, pmg, pfg, w1, w2, pog):
    n_rows = x.shape[0]
    tm = min(ROW_TILE, n_rows)
    row = lambda width: pl.BlockSpec((tm, width), lambda i: (i, 0))
    return pl.pallas_call(
        _mix_ffn_kernel,
        grid=(n_rows // tm,),
        in_specs=[
            row(D_MODEL), row(ATTN_W), row(SG_W),
            _const_spec((D_MODEL, D_MODEL)),
            _const_spec((1, D_MODEL)),
            _const_spec((1, D_MODEL)),
            _const_spec((D_MODEL, D_FF)),
            _const_spec((D_FF, D_MODEL)),
            _const_spec((1, D_MODEL)),
        ],
        out_specs=row(D_MODEL),
        out_shape=jax.ShapeDtypeStruct((n_rows, D_MODEL), F32),
        compiler_params=pltpu.CompilerParams(
            dimension_semantics=("parallel",), vmem_limit_bytes=VMEM_LIMIT),
        name="mix_ffn",
    )(x, an, gn, wo, pmg, pfg, w1, w2, pog)


def _rope_tables_t(seq_len):
    pos = jnp.arange(seq_len)
    r = (pos // GRID_W).astype(F32)
    c = (pos % GRID_W).astype(F32)
    inv = ROPE_THETA ** (-jnp.arange(0, AXIS_DIM, 2, dtype=F32) / AXIS_DIM)
    ang_r = inv[:, None] * r[None, :]
    ang_c = inv[:, None] * c[None, :]
    ang = jnp.concatenate([ang_r, ang_r, ang_c, ang_c], axis=0)
    half = AXIS_DIM // 2
    sign = jnp.concatenate([-jnp.ones((half, 1), F32), jnp.ones((half, 1), F32)] * 2, axis=0)
    return jnp.cos(ang), jnp.sin(ang) * sign


def _trunk(x, p):
    B, S, _ = x.shape
    cos_t, sin_t = _rope_tables_t(S)
    depth = p["w_in"].shape[0]
    for l in range(depth):
        qt, k, ksq, vt, gn = _proj_call(
            x, p["pre_mix_g"][l][None, :], p["w_in"][l], p["q_norm_g"][l][:, None], p["k_norm_g"][l][:, None],
            cos_t, sin_t, p["sg_norm_g"][l][None, :], p["sg_norm_b"][l][None, :], p["sg_w"][l], p["sg_b"][l],
            p["sg_out_g"][l][None, :])
        an = _attn_call(qt, k, ksq, vt, p["attn_out_g"][l][None, :])
        x = _mix_ffn_call(
            x.reshape(B * S, D_MODEL), an.reshape(B * S, ATTN_W), gn.reshape(B * S, SG_W), p["w_out"][l],
            p["post_mix_g"][l][None, :], p["pre_ffn_g"][l][None, :], p["w_ff1"][l], p["w_ff2"][l],
            p["post_ffn_g"][l][None, :]).reshape(B, S, D_MODEL)
    return x


def kernel(x_prompt, x_sample, w_in, w_out, q_norm_g, k_norm_g, sg_norm_g, sg_norm_b, sg_w, sg_b, attn_out_g,
           sg_out_g, pre_mix_g, post_mix_g, pre_ffn_g, post_ffn_g, w_ff1, w_ff2):
    depth = w_in.shape[0]
    p = dict(
        w_in=w_in.astype(BF), w_out=w_out.astype(BF), w_ff1=w_ff1.astype(BF), w_ff2=w_ff2.astype(BF),
        q_norm_g=q_norm_g, k_norm_g=k_norm_g, sg_norm_g=sg_norm_g, sg_norm_b=sg_norm_b,
        sg_w=sg_w.astype(BF).reshape(depth, SG_PAIRS, 2 * CHUNK, CHUNK),
        sg_b=jnp.repeat(jnp.swapaxes(sg_b, 1, 2), SG_W // SG_GROUPS, axis=2),
        attn_out_g=attn_out_g, sg_out_g=sg_out_g, pre_mix_g=pre_mix_g, post_mix_g=post_mix_g,
        pre_ffn_g=pre_ffn_g, post_ffn_g=post_ffn_g,
    )
    return (_trunk(x_prompt, p), _trunk(x_sample, p))
```

```python
import functools

import jax
import jax.numpy as jnp
from jax import lax
from jax.experimental import pallas as pl
from jax.experimental.pallas import tpu as pltpu

D_MODEL = 1024
HEAD_DIM = 64
N_HEADS = 8
N_KV_HEADS = 2
GROUP = N_HEADS // N_KV_HEADS
ATTN_W = N_HEADS * HEAD_DIM
KV_W = N_KV_HEADS * HEAD_DIM
QKV_W = ATTN_W + 2 * KV_W
SG_W = 512
SG_GROUPS = 8
SG_PAIRS = SG_GROUPS // 2
CHUNK = 128
GRID_W = 64
AXIS_DIM = HEAD_DIM // 2
ROPE_THETA = 10000.0
D_FF = 4 * D_MODEL
D_IN = QKV_W + 2 * SG_W
EPS = 1e-6

LANES = 128
SUBLANES = 8

PROJ_TILE = 1024
KEY_CHUNK = 512
Q_TILE = 256
FF_CHUNK = 1024
FFN_TILE = 1024
FFN_SUB = 256
PROJ_SUB = 128
PROJ_AHEAD = 8
S_AHEAD = 4
ITEM_KEYS = 256
KV_UNROLL = 8
LOG2_E = 1.4426950408889634
MIN_DENOM = 2.0 ** -60
VMEM_LIMIT = 56 * 1024 * 1024

BF = jnp.bfloat16
F32 = jnp.float32


def _rms(x, g):
    ms = jnp.mean(x * x, axis=-1, keepdims=True)
    return x * lax.rsqrt(ms + EPS) * g


def _gelu_tanh(x):
    c = 0.7978845608028654
    return 0.5 * x * (1.0 + jnp.tanh(c * (x + 0.044715 * (x * x * x))))


def _const_spec(shape):
    nd = len(shape)
    return pl.BlockSpec(shape, lambda *_: (0,) * nd, pipeline_mode=pl.Buffered(1))


def _proj_kernel(x_ref, g_ref, w_ref, qg_ref, kg_ref, cos_ref, sin_ref, lng_ref, lnb_ref, ws_ref, bs_ref,
                 sgo_ref, qt_ref, k_ref, ksq_ref, vt_ref, gn_ref):
    tm = x_ref.shape[1]
    sub = min(PROJ_SUB, tm)
    nsub = tm // sub

    def project(i):
        h = _rms(x_ref[0, i * sub:(i + 1) * sub], g_ref[...]).astype(BF)
        return jnp.dot(h, w_ref[...], preferred_element_type=F32)

    ahead = [project(i) for i in range(min(PROJ_AHEAD, nsub))]
    for i in range(nsub):
        if i + PROJ_AHEAD < nsub:
            ahead.append(project(i + PROJ_AHEAD))
        _proj_tail(ahead.pop(0), i * sub, sub, qg_ref, kg_ref, cos_ref, sin_ref, lng_ref, lnb_ref, ws_ref, bs_ref,
                   sgo_ref, qt_ref, k_ref, ksq_ref, vt_ref, gn_ref)


def _proj_tail(z, r0, sub, qg_ref, kg_ref, cos_ref, sin_ref, lng_ref, lnb_ref, ws_ref, bs_ref, sgo_ref,
               qt_ref, k_ref, ksq_ref, vt_ref, gn_ref):
    rows = slice(r0, r0 + sub)
    zt = z[:, :QKV_W].T
    cos = cos_ref[:, rows]
    sin = sin_ref[:, rows]

    def norm_rope(blk, gcol):
        ms = jnp.mean(blk * blk, axis=0, keepdims=True)
        y = blk * lax.rsqrt(ms + EPS) * gcol
        half = AXIS_DIM // 2
        sw = jnp.concatenate([y[half:2 * half], y[0:half], y[3 * half:4 * half], y[2 * half:3 * half]], axis=0)
        return y * cos + sw * sin

    zero_blk = jnp.zeros((HEAD_DIM, sub), BF)
    for hh in range(N_HEADS):
        qh = norm_rope(zt[hh * HEAD_DIM:(hh + 1) * HEAD_DIM], qg_ref[...]) * (HEAD_DIM ** -0.5 * LOG2_E)
        c = hh // GROUP
        for cc in range(N_KV_HEADS):
            qt_ref[0, hh, cc * HEAD_DIM:(cc + 1) * HEAD_DIM, rows] = qh.astype(BF) if cc == c else zero_blk
    kt = jnp.concatenate(
        [norm_rope(zt[ATTN_W + c * HEAD_DIM:ATTN_W + (c + 1) * HEAD_DIM], kg_ref[...]) for c in range(N_KV_HEADS)],
        axis=0)
    k_ref[0, rows] = kt.T.astype(BF)
    kf = kt.astype(BF).astype(F32)
    kf = kf * kf
    ksq_ref[0, :, rows] = jnp.concatenate(
        [jnp.sum(kf[c * HEAD_DIM:(c + 1) * HEAD_DIM], axis=0, keepdims=True) for c in range(N_KV_HEADS)], axis=0)
    for c in range(N_KV_HEADS):
        v0 = ATTN_W + KV_W + c * HEAD_DIM
        kc = vt_ref.shape[4]
        vt_ref[0, c, r0 // kc, :, r0 % kc:r0 % kc + sub] = zt[v0:v0 + HEAD_DIM].astype(BF)

    gz = _gelu_tanh(z[:, QKV_W:])
    u = gz[:, :SG_W]
    vv = gz[:, SG_W:]
    mu = jnp.mean(vv, axis=-1, keepdims=True)
    xc = vv - mu
    var = jnp.mean(xc * xc, axis=-1, keepdims=True)
    vb = (xc * lax.rsqrt(var + EPS) * lng_ref[...] + lnb_ref[...]).astype(BF)
    nch = sub // CHUNK
    low_half = lax.broadcasted_iota(jnp.int32, (CHUNK, LANES), 1) < SG_W // SG_GROUPS
    pair_cols = []
    for j in range(SG_PAIRS):
        rhs = jnp.concatenate(
            [vb[c * CHUNK:(c + 1) * CHUNK, j * LANES:(j + 1) * LANES] for c in range(nch)], axis=1)
        r = jnp.dot(ws_ref[j], rhs, preferred_element_type=F32)
        pair_cols.append(jnp.concatenate(
            [jnp.where(low_half, r[:CHUNK, c * LANES:(c + 1) * LANES], r[CHUNK:, c * LANES:(c + 1) * LANES])
             for c in range(nch)], axis=0))
    bias = jnp.concatenate([bs_ref[...]] * nch, axis=0)
    mixed = jnp.concatenate(pair_cols, axis=1) + bias
    gn_ref[0, rows] = _rms(u * mixed, sgo_ref[...]).astype(BF)


def _proj_call(x, g, w, qg, kg, cos_t, sin_t, lng, lnb, ws, bs, sgo):
    B, S, _ = x.shape
    tm = min(PROJ_TILE, S)
    n = S // tm
    kc = min(KEY_CHUNK, tm)
    out_shape = (
        jax.ShapeDtypeStruct((B, N_HEADS, KV_W, S), BF),
        jax.ShapeDtypeStruct((B, S, KV_W), BF),
        jax.ShapeDtypeStruct((B, N_KV_HEADS, S), F32),
        jax.ShapeDtypeStruct((B, N_KV_HEADS, S // kc, HEAD_DIM, kc), BF),
        jax.ShapeDtypeStruct((B, S, SG_W), BF),
    )
    return pl.pallas_call(
        _proj_kernel,
        grid=(B, n),
        in_specs=[
            pl.BlockSpec((1, tm, D_MODEL), lambda b, i: (b, i, 0)),
            _const_spec((1, D_MODEL)),
            _const_spec((D_MODEL, D_IN)),
            _const_spec((HEAD_DIM, 1)),
            _const_spec((HEAD_DIM, 1)),
            pl.BlockSpec((HEAD_DIM, tm), lambda b, i: (0, i)),
            pl.BlockSpec((HEAD_DIM, tm), lambda b, i: (0, i)),
            _const_spec((1, SG_W)),
            _const_spec((1, SG_W)),
            _const_spec((SG_PAIRS, 2 * CHUNK, CHUNK)),
            _const_spec((CHUNK, SG_W)),
            _const_spec((1, SG_W)),
        ],
        out_specs=(
            pl.BlockSpec((1, N_HEADS, KV_W, tm), lambda b, i: (b, 0, 0, i)),
            pl.BlockSpec((1, tm, KV_W), lambda b, i: (b, i, 0)),
            pl.BlockSpec((1, N_KV_HEADS, tm), lambda b, i: (b, 0, i)),
            pl.BlockSpec((1, N_KV_HEADS, tm // kc, HEAD_DIM, kc), lambda b, i: (b, 0, i, 0, 0)),
            pl.BlockSpec((1, tm, SG_W), lambda b, i: (b, i, 0)),
        ),
        out_shape=out_shape,
        compiler_params=pltpu.CompilerParams(
            dimension_semantics=("parallel", "parallel"), vmem_limit_bytes=VMEM_LIMIT),
        name="proj",
    )(x, g, w, qg, kg, cos_t, sin_t, lng, lnb, ws, bs, sgo)


def _attn_kernel(qt_ref, k_ref, ksq_ref, vt_ref, ag_ref, o_ref, stab_ref, acc_ref, den_ref, sbuf_ref):
    nkv = vt_ref.shape[2]
    tk = vt_ref.shape[4]
    unroll = KV_UNROLL if nkv % KV_UNROLL == 0 else 1

    ik = min(ITEM_KEYS, tk)

    def k_chunk(j):
        return k_ref[0, pl.ds(pl.multiple_of(j * tk, tk), tk), :]

    def k_item(j, s):
        start = j * tk + s * ik
        if not isinstance(start, int):
            start = pl.multiple_of(start, ik)
        return k_ref[0, pl.ds(start, ik), :]

    def sweep():
        acc_ref[...] = jnp.zeros(acc_ref.shape, F32)
        den_ref[...] = jnp.zeros(den_ref.shape, F32)

        n_pass = nkv // unroll
        pass_items = lambda jj: [(jj * unroll + c, s, h)
                                 for c in range(unroll) for s in range(tk // ik) for h in range(N_HEADS)]
        score = lambda j, s, h: jnp.dot(k_item(j, s), qt_ref[0, h], preferred_element_type=F32)
        for a, it in enumerate(pass_items(0)[:S_AHEAD]):
            sbuf_ref[a] = score(*it)

        def body(jj, carry):
            items = pass_items(jj)
            items += pass_items(jnp.minimum(jj + 1, n_pass - 1))[:S_AHEAD]
            ahead = []
            for n, (j, s, h) in enumerate(items[:-S_AHEAD]):
                nxt = score(*items[n + S_AHEAD])
                if n + S_AHEAD < len(items) - S_AHEAD:
                    ahead.append(nxt)
                else:
                    sbuf_ref[n + 2 * S_AHEAD - len(items)] = nxt
                st = sbuf_ref[n] if n < S_AHEAD else ahead.pop(0)
                p = jnp.exp2(st - stab_ref[h])
                den_ref[h] += jnp.sum(p.reshape(ik // SUBLANES, SUBLANES, p.shape[1]), axis=0)
                acc_ref[h] += jnp.dot(vt_ref[0, h // GROUP, j, :, s * ik:(s + 1) * ik], p.astype(BF),
                                      preferred_element_type=F32)
            return carry

        lax.fori_loop(0, n_pass, body, 0)
        return jnp.min(jnp.sum(den_ref[...], axis=1))

    kmax = jnp.sqrt(jnp.max(ksq_ref[0], axis=1, keepdims=True))
    for h in range(N_HEADS):
        q = qt_ref[0, h].astype(F32)
        qn = jnp.sqrt(jnp.sum(q * q, axis=0, keepdims=True))
        stab_ref[h] = qn * kmax[h // GROUP:h // GROUP + 1]
    min_denom = sweep()

    @pl.when(min_denom < MIN_DENOM)
    def _():
        stab_ref[...] = jnp.full(stab_ref.shape, -jnp.inf, F32)

        def body(j, carry):
            kc = k_chunk(j)
            for h in range(N_HEADS):
                st = jnp.dot(kc, qt_ref[0, h], preferred_element_type=F32)
                stab_ref[h] = jnp.maximum(stab_ref[h], jnp.max(st, axis=0, keepdims=True))
            return carry

        lax.fori_loop(0, nkv, body, 0)
        sweep()

    outs = []
    for h in range(N_HEADS):
        outs.append(acc_ref[h] / jnp.sum(den_ref[h], axis=0, keepdims=True))
    o = jnp.concatenate(outs, axis=0).T
    o_ref[0] = _rms(o, ag_ref[...]).astype(BF)


def _attn_call(qt, k, ksq, vt, ag):
    B, _, _, S = qt.shape
    nkv, tk = vt.shape[2], vt.shape[4]
    tq = min(Q_TILE, S)
    return pl.pallas_call(
        _attn_kernel,
        grid=(B, S // tq),
        in_specs=[
            pl.BlockSpec((1, N_HEADS, KV_W, tq), lambda b, i: (b, 0, 0, i)),
            pl.BlockSpec((1, S, KV_W), lambda b, i: (b, 0, 0)),
            pl.BlockSpec((1, N_KV_HEADS, S), lambda b, i: (b, 0, 0)),
            pl.BlockSpec((1, N_KV_HEADS, nkv, HEAD_DIM, tk), lambda b, i: (b, 0, 0, 0, 0)),
            _const_spec((1, ATTN_W)),
        ],
        out_specs=pl.BlockSpec((1, tq, ATTN_W), lambda b, i: (b, i, 0)),
        out_shape=jax.ShapeDtypeStruct((B, S, ATTN_W), BF),
        scratch_shapes=[
            pltpu.VMEM((N_HEADS, 1, tq), F32),
            pltpu.VMEM((N_HEADS, HEAD_DIM, tq), F32),
            pltpu.VMEM((N_HEADS, SUBLANES, tq), F32),
            pltpu.VMEM((S_AHEAD, min(ITEM_KEYS, tk), tq), F32),
        ],
        compiler_params=pltpu.CompilerParams(
            dimension_semantics=("parallel", "arbitrary"), vmem_limit_bytes=VMEM_LIMIT),
        name="attn",
    )(qt, k, ksq, vt, ag)


def _mix_ffn_kernel(x_ref, an_ref, gn_ref, wo_ref, pmg_ref, pfg_ref, w1_ref, w2_ref, pog_ref, o_ref):
    tm = x_ref.shape[0]
    sub = min(FFN_SUB, tm)
    mixes = []
    for r0 in range(0, tm, sub):
        cat = jnp.concatenate([an_ref[r0:r0 + sub], gn_ref[r0:r0 + sub]], axis=-1)
        mixes.append(jnp.dot(cat, wo_ref[...], preferred_element_type=F32))
    mids = []
    for i, m in enumerate(mixes):
        x1 = x_ref[i * sub:(i + 1) * sub] + _rms(m, pmg_ref[...])
        mids.append((x1, _rms(x1, pfg_ref[...]).astype(BF)))
    for i, (x1, h) in enumerate(mids):
        f = None
        for c in range(D_FF // FF_CHUNK):
            a = jnp.dot(h, w1_ref[:, c * FF_CHUNK:(c + 1) * FF_CHUNK], preferred_element_type=F32)
            a = jnp.maximum(a, 0.0)
            a = (a * a).astype(BF)
            part = jnp.dot(a, w2_ref[c * FF_CHUNK:(c + 1) * FF_CHUNK, :], preferred_element_type=F32)
            f = part if f is None else f + part
        o_ref[i * sub:(i + 1) * sub] = x1 + _rms(f, pog_ref[...])


def _mix_ffn_call(x, an, gn, wo, pmg, pfg, w1, w2, pog):
    n_rows = x.shape[0]
    tm = min(FFN_TILE, n_rows)
    row = lambda width: pl.BlockSpec((tm, width), lambda i: (i, 0))
    return pl.pallas_call(
        _mix_ffn_kernel,
        grid=(n_rows // tm,),
        in_specs=[
            row(D_MODEL), row(ATTN_W), row(SG_W),
            _const_spec((D_MODEL, D_MODEL)),
            _const_spec((1, D_MODEL)),
            _const_spec((1, D_MODEL)),
            _const_spec((D_MODEL, D_FF)),
            _const_spec((D_FF, D_MODEL)),
            _const_spec((1, D_MODEL)),
        ],
        out_specs=row(D_MODEL),
        out_shape=jax.ShapeDtypeStruct((n_rows, D_MODEL), F32),
        compiler_params=pltpu.CompilerParams(
            dimension_semantics=("parallel",), vmem_limit_bytes=VMEM_LIMIT),
        name="mix_ffn",
    )(x, an, gn, wo, pmg, pfg, w1, w2, pog)


def _rope_tables_t(seq_len):
    pos = jnp.arange(seq_len)
    r = (pos // GRID_W).astype(F32)
    c = (pos % GRID_W).astype(F32)
    inv = ROPE_THETA ** (-jnp.arange(0, AXIS_DIM, 2, dtype=F32) / AXIS_DIM)
    ang_r = inv[:, None] * r[None, :]
    ang_c = inv[:, None] * c[None, :]
    ang = jnp.concatenate([ang_r, ang_r, ang_c, ang_c], axis=0)
    half = AXIS_DIM // 2
    sign = jnp.concatenate([-jnp.ones((half, 1), F32), jnp.ones((half, 1), F32)] * 2, axis=0)
    return jnp.cos(ang), jnp.sin(ang) * sign


def _trunk(x, p):
    B, S, _ = x.shape
    cos_t, sin_t = _rope_tables_t(S)
    depth = p["w_in"].shape[0]
    for l in range(depth):
        qt, k, ksq, vt, gn = _proj_call(
            x, p["pre_mix_g"][l][None, :], p["w_in"][l], p["q_norm_g"][l][:, None], p["k_norm_g"][l][:, None],
            cos_t, sin_t, p["sg_norm_g"][l][None, :], p["sg_norm_b"][l][None, :], p["sg_w"][l], p["sg_b"][l],
            p["sg_out_g"][l][None, :])
        an = _attn_call(qt, k, ksq, vt, p["attn_out_g"][l][None, :])
        x = _mix_ffn_call(
            x.reshape(B * S, D_MODEL), an.reshape(B * S, ATTN_W), gn.reshape(B * S, SG_W), p["w_out"][l],
            p["post_mix_g"][l][None, :], p["pre_ffn_g"][l][None, :], p["w_ff1"][l], p["w_ff2"][l],
            p["post_ffn_g"][l][None, :]).reshape(B, S, D_MODEL)
    return x


def kernel(x_prompt, x_sample, w_in, w_out, q_norm_g, k_norm_g, sg_norm_g, sg_norm_b, sg_w, sg_b, attn_out_g,
           sg_out_g, pre_mix_g, post_mix_g, pre_ffn_g, post_ffn_g, w_ff1, w_ff2):
    depth = w_in.shape[0]
    p = dict(
        w_in=w_in.astype(BF), w_out=w_out.astype(BF), w_ff1=w_ff1.astype(BF), w_ff2=w_ff2.astype(BF),
        q_norm_g=q_norm_g, k_norm_g=k_norm_g, sg_norm_g=sg_norm_g, sg_norm_b=sg_norm_b,
        sg_w=sg_w.astype(BF).reshape(depth, SG_PAIRS, 2 * CHUNK, CHUNK),
        sg_b=jnp.repeat(jnp.swapaxes(sg_b, 1, 2), SG_W // SG_GROUPS, axis=2),
        attn_out_g=attn_out_g, sg_out_g=sg_out_g, pre_mix_g=pre_mix_g, post_mix_g=post_mix_g,
        pre_ffn_g=pre_ffn_g, post_ffn_g=post_ffn_g,
    )
    return (_trunk(x_prompt, p), _trunk(x_sample, p))
```

```python
import functools

import jax
import jax.numpy as jnp
from jax import lax
from jax.experimental import pallas as pl
from jax.experimental.pallas import tpu as pltpu

D_MODEL = 1024
HEAD_DIM = 64
N_HEADS = 8
N_KV_HEADS = 2
GROUP = N_HEADS // N_KV_HEADS
ATTN_W = N_HEADS * HEAD_DIM
KV_W = N_KV_HEADS * HEAD_DIM
QKV_W = ATTN_W + 2 * KV_W
SG_W = 512
SG_GROUPS = 8
SG_PAIRS = SG_GROUPS // 2
CHUNK = 128
GRID_W = 64
AXIS_DIM = HEAD_DIM // 2
ROPE_THETA = 10000.0
D_FF = 4 * D_MODEL
D_IN = QKV_W + 2 * SG_W
EPS = 1e-6

LANES = 128
SUBLANES = 8

PROJ_TILE = 1024
KEY_CHUNK = 512
Q_TILE = 256
FF_CHUNK = 1024
FFN_TILE = 1024
FFN_SUB = 256
PROJ_SUB = 128
PROJ_AHEAD = 8
S_AHEAD = 4
ITEM_KEYS = 256
KV_UNROLL = 8
LOG2_E = 1.4426950408889634
MIN_DENOM = 2.0 ** -60
VMEM_LIMIT = 56 * 1024 * 1024

BF = jnp.bfloat16
F32 = jnp.float32


def _rms(x, g):
    ms = jnp.mean(x * x, axis=-1, keepdims=True)
    return x * lax.rsqrt(ms + EPS) * g


def _gelu_tanh(x):
    c = 0.7978845608028654
    return 0.5 * x * (1.0 + jnp.tanh(c * (x + 0.044715 * (x * x * x))))


def _layer_spec(shape, l):
    nd = len(shape)
    return pl.BlockSpec((None,) + tuple(shape), lambda *_: (l,) + (0,) * nd, pipeline_mode=pl.Buffered(1))


def _proj_kernel(x_ref, g_ref, w_ref, qg_ref, kg_ref, cos_ref, sin_ref, lng_ref, lnb_ref, ws_ref, bs_ref,
                 sgo_ref, qt_ref, k_ref, ksq_ref, vt_ref, gn_ref):
    tm = x_ref.shape[1]
    sub = min(PROJ_SUB, tm)
    nsub = tm // sub

    def project(i):
        h = _rms(x_ref[0, i * sub:(i + 1) * sub], g_ref[...]).astype(BF)
        return jnp.dot(h, w_ref[...], preferred_element_type=F32)

    ahead = [project(i) for i in range(min(PROJ_AHEAD, nsub))]
    for i in range(nsub):
        if i + PROJ_AHEAD < nsub:
            ahead.append(project(i + PROJ_AHEAD))
        _proj_tail(ahead.pop(0), i * sub, sub, qg_ref, kg_ref, cos_ref, sin_ref, lng_ref, lnb_ref, ws_ref, bs_ref,
                   sgo_ref, qt_ref, k_ref, ksq_ref, vt_ref, gn_ref)


def _proj_tail(z, r0, sub, qg_ref, kg_ref, cos_ref, sin_ref, lng_ref, lnb_ref, ws_ref, bs_ref, sgo_ref,
               qt_ref, k_ref, ksq_ref, vt_ref, gn_ref):
    rows = slice(r0, r0 + sub)
    zt = z[:, :QKV_W].T
    cos = cos_ref[:, rows]
    sin = sin_ref[:, rows]

    def norm_rope(blk, gcol):
        ms = jnp.mean(blk * blk, axis=0, keepdims=True)
        y = blk * lax.rsqrt(ms + EPS) * gcol
        half = AXIS_DIM // 2
        sw = jnp.concatenate([y[half:2 * half], y[0:half], y[3 * half:4 * half], y[2 * half:3 * half]], axis=0)
        return y * cos + sw * sin

    zero_blk = jnp.zeros((HEAD_DIM, sub), BF)
    for hh in range(N_HEADS):
        qh = norm_rope(zt[hh * HEAD_DIM:(hh + 1) * HEAD_DIM], qg_ref[...]) * (HEAD_DIM ** -0.5 * LOG2_E)
        c = hh // GROUP
        for cc in range(N_KV_HEADS):
            qt_ref[0, hh, cc * HEAD_DIM:(cc + 1) * HEAD_DIM, rows] = qh.astype(BF) if cc == c else zero_blk
    kt = jnp.concatenate(
        [norm_rope(zt[ATTN_W + c * HEAD_DIM:ATTN_W + (c + 1) * HEAD_DIM], kg_ref[...]) for c in range(N_KV_HEADS)],
        axis=0)
    k_ref[0, rows] = kt.T.astype(BF)
    kf = kt.astype(BF).astype(F32)
    kf = kf * kf
    ksq_ref[0, :, rows] = jnp.concatenate(
        [jnp.sum(kf[c * HEAD_DIM:(c + 1) * HEAD_DIM], axis=0, keepdims=True) for c in range(N_KV_HEADS)], axis=0)
    for c in range(N_KV_HEADS):
        v0 = ATTN_W + KV_W + c * HEAD_DIM
        kc = vt_ref.shape[4]
        vt_ref[0, c, r0 // kc, :, r0 % kc:r0 % kc + sub] = zt[v0:v0 + HEAD_DIM].astype(BF)

    gz = _gelu_tanh(z[:, QKV_W:])
    u = gz[:, :SG_W]
    vv = gz[:, SG_W:]
    mu = jnp.mean(vv, axis=-1, keepdims=True)
    xc = vv - mu
    var = jnp.mean(xc * xc, axis=-1, keepdims=True)
    vb = (xc * lax.rsqrt(var + EPS) * lng_ref[...] + lnb_ref[...]).astype(BF)
    nch = sub // CHUNK
    low_half = lax.broadcasted_iota(jnp.int32, (CHUNK, LANES), 1) < SG_W // SG_GROUPS
    pair_cols = []
    for j in range(SG_PAIRS):
        rhs = jnp.concatenate(
            [vb[c * CHUNK:(c + 1) * CHUNK, j * LANES:(j + 1) * LANES] for c in range(nch)], axis=1)
        r = jnp.dot(ws_ref[j], rhs, preferred_element_type=F32)
        pair_cols.append(jnp.concatenate(
            [jnp.where(low_half, r[:CHUNK, c * LANES:(c + 1) * LANES], r[CHUNK:, c * LANES:(c + 1) * LANES])
             for c in range(nch)], axis=0))
    bias = jnp.concatenate([bs_ref[...]] * nch, axis=0)
    mixed = jnp.concatenate(pair_cols, axis=1) + bias
    gn_ref[0, rows] = _rms(u * mixed, sgo_ref[...]).astype(BF)


def _proj_call(x, l, p, cos_t, sin_t):
    B, S, _ = x.shape
    tm = min(PROJ_TILE, S)
    n = S // tm
    kc = min(KEY_CHUNK, tm)
    out_shape = (
        jax.ShapeDtypeStruct((B, N_HEADS, KV_W, S), BF),
        jax.ShapeDtypeStruct((B, S, KV_W), BF),
        jax.ShapeDtypeStruct((B, N_KV_HEADS, S), F32),
        jax.ShapeDtypeStruct((B, N_KV_HEADS, S // kc, HEAD_DIM, kc), BF),
        jax.ShapeDtypeStruct((B, S, SG_W), BF),
    )
    return pl.pallas_call(
        _proj_kernel,
        grid=(B, n),
        in_specs=[
            pl.BlockSpec((1, tm, D_MODEL), lambda b, i: (b, i, 0)),
            _layer_spec((1, D_MODEL), l),
            _layer_spec((D_MODEL, D_IN), l),
            _layer_spec((HEAD_DIM, 1), l),
            _layer_spec((HEAD_DIM, 1), l),
            pl.BlockSpec((HEAD_DIM, tm), lambda b, i: (0, i)),
            pl.BlockSpec((HEAD_DIM, tm), lambda b, i: (0, i)),
            _layer_spec((1, SG_W), l),
            _layer_spec((1, SG_W), l),
            _layer_spec((SG_PAIRS, 2 * CHUNK, CHUNK), l),
            _layer_spec((CHUNK, SG_W), l),
            _layer_spec((1, SG_W), l),
        ],
        out_specs=(
            pl.BlockSpec((1, N_HEADS, KV_W, tm), lambda b, i: (b, 0, 0, i)),
            pl.BlockSpec((1, tm, KV_W), lambda b, i: (b, i, 0)),
            pl.BlockSpec((1, N_KV_HEADS, tm), lambda b, i: (b, 0, i)),
            pl.BlockSpec((1, N_KV_HEADS, tm // kc, HEAD_DIM, kc), lambda b, i: (b, 0, i, 0, 0)),
            pl.BlockSpec((1, tm, SG_W), lambda b, i: (b, i, 0)),
        ),
        out_shape=out_shape,
        compiler_params=pltpu.CompilerParams(
            dimension_semantics=("parallel", "parallel"), vmem_limit_bytes=VMEM_LIMIT),
        name="proj",
    )(x, p["pre_mix_g"], p["w_in"], p["q_norm_g"], p["k_norm_g"], cos_t, sin_t, p["sg_norm_g"], p["sg_norm_b"],
      p["sg_w"], p["sg_b"], p["sg_out_g"])


def _attn_kernel(qt_ref, k_ref, ksq_ref, vt_ref, ag_ref, o_ref, stab_ref, acc_ref, den_ref, sbuf_ref):
    nkv = vt_ref.shape[2]
    tk = vt_ref.shape[4]
    unroll = KV_UNROLL if nkv % KV_UNROLL == 0 else 1

    ik = min(ITEM_KEYS, tk)

    def k_chunk(j):
        return k_ref[0, pl.ds(pl.multiple_of(j * tk, tk), tk), :]

    def k_item(j, s):
        start = j * tk + s * ik
        if not isinstance(start, int):
            start = pl.multiple_of(start, ik)
        return k_ref[0, pl.ds(start, ik), :]

    n_pass = nkv // unroll
    pass_items = lambda jj: [(jj * unroll + c, s, h)
                             for c in range(unroll) for s in range(tk // ik) for h in range(N_HEADS)]
    score = lambda j, s, h: jnp.dot(k_item(j, s), qt_ref[0, h], preferred_element_type=F32)

    def first_scores():
        for a, it in enumerate(pass_items(0)[:S_AHEAD]):
            sbuf_ref[a] = score(*it)

    def sweep():
        acc_ref[...] = jnp.zeros(acc_ref.shape, F32)
        den_ref[...] = jnp.zeros(den_ref.shape, F32)

        def body(jj, carry):
            items = pass_items(jj)
            items += pass_items(jnp.minimum(jj + 1, n_pass - 1))[:S_AHEAD]
            ahead = []
            for n, (j, s, h) in enumerate(items[:-S_AHEAD]):
                nxt = score(*items[n + S_AHEAD])
                if n + S_AHEAD < len(items) - S_AHEAD:
                    ahead.append(nxt)
                else:
                    sbuf_ref[n + 2 * S_AHEAD - len(items)] = nxt
                st = sbuf_ref[n] if n < S_AHEAD else ahead.pop(0)
                p = jnp.exp2(st - stab_ref[h])
                den_ref[h] += jnp.sum(p.reshape(ik // SUBLANES, SUBLANES, p.shape[1]), axis=0)
                acc_ref[h] += jnp.dot(vt_ref[0, h // GROUP, j, :, s * ik:(s + 1) * ik], p.astype(BF),
                                      preferred_element_type=F32)
            return carry

        lax.fori_loop(0, n_pass, body, 0)
        return jnp.min(jnp.sum(den_ref[...], axis=1))

    first_scores()
    kmax = jnp.sqrt(jnp.max(ksq_ref[0], axis=1, keepdims=True))
    for h in range(N_HEADS):
        q = qt_ref[0, h].astype(F32)
        qn = jnp.sqrt(jnp.sum(q * q, axis=0, keepdims=True))
        stab_ref[h] = qn * kmax[h // GROUP:h // GROUP + 1]
    min_denom = sweep()

    @pl.when(min_denom < MIN_DENOM)
    def _():
        stab_ref[...] = jnp.full(stab_ref.shape, -jnp.inf, F32)

        def body(j, carry):
            kc = k_chunk(j)
            for h in range(N_HEADS):
                st = jnp.dot(kc, qt_ref[0, h], preferred_element_type=F32)
                stab_ref[h] = jnp.maximum(stab_ref[h], jnp.max(st, axis=0, keepdims=True))
            return carry

        lax.fori_loop(0, nkv, body, 0)
        first_scores()
        sweep()

    outs = []
    for h in range(N_HEADS):
        outs.append(acc_ref[h] / jnp.sum(den_ref[h], axis=0, keepdims=True))
    o = jnp.concatenate(outs, axis=0).T
    o_ref[0] = _rms(o, ag_ref[...]).astype(BF)


def _attn_call(qt, k, ksq, vt, l, p):
    B, _, _, S = qt.shape
    nkv, tk = vt.shape[2], vt.shape[4]
    tq = min(Q_TILE, S)
    return pl.pallas_call(
        _attn_kernel,
        grid=(B, S // tq),
        in_specs=[
            pl.BlockSpec((1, N_HEADS, KV_W, tq), lambda b, i: (b, 0, 0, i)),
            pl.BlockSpec((1, S, KV_W), lambda b, i: (b, 0, 0)),
            pl.BlockSpec((1, N_KV_HEADS, S), lambda b, i: (b, 0, 0)),
            pl.BlockSpec((1, N_KV_HEADS, nkv, HEAD_DIM, tk), lambda b, i: (b, 0, 0, 0, 0)),
            _layer_spec((1, ATTN_W), l),
        ],
        out_specs=pl.BlockSpec((1, tq, ATTN_W), lambda b, i: (b, i, 0)),
        out_shape=jax.ShapeDtypeStruct((B, S, ATTN_W), BF),
        scratch_shapes=[
            pltpu.VMEM((N_HEADS, 1, tq), F32),
            pltpu.VMEM((N_HEADS, HEAD_DIM, tq), F32),
            pltpu.VMEM((N_HEADS, SUBLANES, tq), F32),
            pltpu.VMEM((S_AHEAD, min(ITEM_KEYS, tk), tq), F32),
        ],
        compiler_params=pltpu.CompilerParams(
            dimension_semantics=("parallel", "arbitrary"), vmem_limit_bytes=VMEM_LIMIT),
        name="attn",
    )(qt, k, ksq, vt, p["attn_out_g"])


def _mix_ffn_kernel(x_ref, an_ref, gn_ref, wo_ref, pmg_ref, pfg_ref, w1_ref, w2_ref, pog_ref, o_ref):
    tm = x_ref.shape[0]
    sub = min(FFN_SUB, tm)
    mixes = []
    for r0 in range(0, tm, sub):
        cat = jnp.concatenate([an_ref[r0:r0 + sub], gn_ref[r0:r0 + sub]], axis=-1)
        mixes.append(jnp.dot(cat, wo_ref[...], preferred_element_type=F32))
    mids = []
    for i, m in enumerate(mixes):
        x1 = x_ref[i * sub:(i + 1) * sub] + _rms(m, pmg_ref[...])
        mids.append((x1, _rms(x1, pfg_ref[...]).astype(BF)))
    for i, (x1, h) in enumerate(mids):
        f = None
        for c in range(D_FF // FF_CHUNK):
            a = jnp.dot(h, w1_ref[:, c * FF_CHUNK:(c + 1) * FF_CHUNK], preferred_element_type=F32)
            a = jnp.maximum(a, 0.0)
            a = (a * a).astype(BF)
            part = jnp.dot(a, w2_ref[c * FF_CHUNK:(c + 1) * FF_CHUNK, :], preferred_element_type=F32)
            f = part if f is None else f + part
        o_ref[i * sub:(i + 1) * sub] = x1 + _rms(f, pog_ref[...])


def _mix_ffn_call(x, an, gn, l, p):
    n_rows = x.shape[0]
    tm = min(FFN_TILE, n_rows)
    row = lambda width: pl.BlockSpec((tm, width), lambda i: (i, 0))
    return pl.pallas_call(
        _mix_ffn_kernel,
        grid=(n_rows // tm,),
        in_specs=[
            row(D_MODEL), row(ATTN_W), row(SG_W),
            _layer_spec((D_MODEL, D_MODEL), l),
            _layer_spec((1, D_MODEL), l),
            _layer_spec((1, D_MODEL), l),
            _layer_spec((D_MODEL, D_FF), l),
            _layer_spec((D_FF, D_MODEL), l),
            _layer_spec((1, D_MODEL), l),
        ],
        out_specs=row(D_MODEL),
        out_shape=jax.ShapeDtypeStruct((n_rows, D_MODEL), F32),
        compiler_params=pltpu.CompilerParams(
            dimension_semantics=("parallel",), vmem_limit_bytes=VMEM_LIMIT),
        name="mix_ffn",
    )(x, an, gn, p["w_out"], p["post_mix_g"], p["pre_ffn_g"], p["w_ff1"], p["w_ff2"], p["post_ffn_g"])


def _rope_tables_t(seq_len):
    pos = jnp.arange(seq_len)
    r = (pos // GRID_W).astype(F32)
    c = (pos % GRID_W).astype(F32)
    inv = ROPE_THETA ** (-jnp.arange(0, AXIS_DIM, 2, dtype=F32) / AXIS_DIM)
    ang_r = inv[:, None] * r[None, :]
    ang_c = inv[:, None] * c[None, :]
    ang = jnp.concatenate([ang_r, ang_r, ang_c, ang_c], axis=0)
    half = AXIS_DIM // 2
    sign = jnp.concatenate([-jnp.ones((half, 1), F32), jnp.ones((half, 1), F32)] * 2, axis=0)
    return jnp.cos(ang), jnp.sin(ang) * sign


def _trunk(x, p, cos_t, sin_t):
    B, S, _ = x.shape
    for l in range(p["w_in"].shape[0]):
        qt, k, ksq, vt, gn = _proj_call(x, l, p, cos_t, sin_t)
        an = _attn_call(qt, k, ksq, vt, l, p)
        x = _mix_ffn_call(x.reshape(B * S, D_MODEL), an.reshape(B * S, ATTN_W), gn.reshape(B * S, SG_W), l,
                          p).reshape(B, S, D_MODEL)
    return x


def kernel(x_prompt, x_sample, w_in, w_out, q_norm_g, k_norm_g, sg_norm_g, sg_norm_b, sg_w, sg_b, attn_out_g,
           sg_out_g, pre_mix_g, post_mix_g, pre_ffn_g, post_ffn_g, w_ff1, w_ff2):
    depth = w_in.shape[0]
    p = dict(
        w_in=w_in.astype(BF), w_out=w_out.astype(BF), w_ff1=w_ff1.astype(BF), w_ff2=w_ff2.astype(BF),
        q_norm_g=q_norm_g[:, :, None], k_norm_g=k_norm_g[:, :, None],
        sg_norm_g=sg_norm_g[:, None, :], sg_norm_b=sg_norm_b[:, None, :],
        sg_w=sg_w.astype(BF).reshape(depth, SG_PAIRS, 2 * CHUNK, CHUNK),
        sg_b=jnp.repeat(jnp.swapaxes(sg_b, 1, 2), SG_W // SG_GROUPS, axis=2),
        attn_out_g=attn_out_g[:, None, :], sg_out_g=sg_out_g[:, None, :], pre_mix_g=pre_mix_g[:, None, :],
        post_mix_g=post_mix_g[:, None, :], pre_ffn_g=pre_ffn_g[:, None, :], post_ffn_g=post_ffn_g[:, None, :],
    )
    cos_t, sin_t = _rope_tables_t(max(x_prompt.shape[1], x_sample.shape[1]))
    return (_trunk(x_prompt, p, cos_t, sin_t), _trunk(x_sample, p, cos_t, sin_t))
```

```python
import functools
import math

import jax
import jax.numpy as jnp
from jax import lax
from jax.experimental import pallas as pl
from jax.experimental.pallas import tpu as pltpu

D_MODEL = 1024
HEAD_DIM = 64
N_HEADS = 8
N_KV_HEADS = 2
GROUP = N_HEADS // N_KV_HEADS
ATTN_W = N_HEADS * HEAD_DIM
KV_W = N_KV_HEADS * HEAD_DIM
QKV_W = ATTN_W + 2 * KV_W
SG_W = 512
SG_GROUPS = 8
SG_PAIRS = SG_GROUPS // 2
CHUNK = 128
GRID_W = 64
AXIS_DIM = HEAD_DIM // 2
ROPE_THETA = 10000.0
D_FF = 4 * D_MODEL
D_IN = QKV_W + 2 * SG_W
EPS = 1e-6

LANES = 128
SUBLANES = 8

PROJ_TILE = 1024
PROJ_SUB = 128
FFN_TILE = 1024
FFN_SUB = 256
FF_CHUNK = 1024
Q_TILE = 256
KEY_CHUNK = 512
ITEM_KEYS = 256
KV_UNROLL = 32
S_AHEAD = 4
LOG2_E = 1.4426950408889634
MIN_DENOM = 2.0 ** -60
VMEM_LIMIT = 56 * 1024 * 1024

BF = jnp.bfloat16
F32 = jnp.float32


def _rms(x, g):
    ms = jnp.mean(x * x, axis=-1, keepdims=True)
    return x * lax.rsqrt(ms + EPS) * g


def _gelu_tanh(x):
    c = 0.7978845608028654
    return 0.5 * x * (1.0 + jnp.tanh(c * (x + 0.044715 * (x * x * x))))


def _layer_spec(shape, l):
    nd = len(shape)
    return pl.BlockSpec((None,) + tuple(shape), lambda *_: (l,) + (0,) * nd, pipeline_mode=pl.Buffered(1))


def _proj_kernel(x_ref, g_ref, w_ref, qg_ref, kg_ref, cos_ref, sin_ref, lng_ref, lnb_ref, ws_ref, bs_ref,
                 sgo_ref, qt_ref, k_ref, ksq_ref, vt_ref, gn_ref):
    tm = x_ref.shape[1]
    sub = min(PROJ_SUB, tm)
    zs = []
    for r0 in range(0, tm, sub):
        h = _rms(x_ref[0, r0:r0 + sub], g_ref[...]).astype(BF)
        zs.append(jnp.dot(h, w_ref[...], preferred_element_type=F32))
    for i, z in enumerate(zs):
        _proj_tail(z, i * sub, sub, qg_ref, kg_ref, cos_ref, sin_ref, lng_ref, lnb_ref, ws_ref, bs_ref, sgo_ref,
                   qt_ref, k_ref, ksq_ref, vt_ref, gn_ref)


def _proj_tail(z, r0, sub, qg_ref, kg_ref, cos_ref, sin_ref, lng_ref, lnb_ref, ws_ref, bs_ref, sgo_ref,
               qt_ref, k_ref, ksq_ref, vt_ref, gn_ref):
    rows = slice(r0, r0 + sub)
    zt = z[:, :QKV_W].T
    cos = cos_ref[:, rows]
    sin = sin_ref[:, rows]

    def norm_rope(blk, gcol):
        ms = jnp.mean(blk * blk, axis=0, keepdims=True)
        y = blk * lax.rsqrt(ms + EPS) * gcol
        half = AXIS_DIM // 2
        sw = jnp.concatenate([y[half:2 * half], y[0:half], y[3 * half:4 * half], y[2 * half:3 * half]], axis=0)
        return y * cos + sw * sin

    zero_blk = jnp.zeros((HEAD_DIM, sub), BF)
    for hh in range(N_HEADS):
        qh = norm_rope(zt[hh * HEAD_DIM:(hh + 1) * HEAD_DIM], qg_ref[...]) * (HEAD_DIM ** -0.5 * LOG2_E)
        c = hh // GROUP
        for cc in range(N_KV_HEADS):
            qt_ref[0, hh, cc * HEAD_DIM:(cc + 1) * HEAD_DIM, rows] = qh.astype(BF) if cc == c else zero_blk
    kt = jnp.concatenate(
        [norm_rope(zt[ATTN_W + c * HEAD_DIM:ATTN_W + (c + 1) * HEAD_DIM], kg_ref[...]) for c in range(N_KV_HEADS)],
        axis=0)
    k_ref[0, rows] = kt.T.astype(BF)
    kf = kt.astype(BF).astype(F32)
    kf = kf * kf
    ksq_ref[0, :, rows] = jnp.concatenate(
        [jnp.sum(kf[c * HEAD_DIM:(c + 1) * HEAD_DIM], axis=0, keepdims=True) for c in range(N_KV_HEADS)], axis=0)
    for c in range(N_KV_HEADS):
        v0 = ATTN_W + KV_W + c * HEAD_DIM
        kc = vt_ref.shape[4]
        vt_ref[0, c, r0 // kc, :, r0 % kc:r0 % kc + sub] = zt[v0:v0 + HEAD_DIM].astype(BF)

    gz = _gelu_tanh(z[:, QKV_W:])
    u = gz[:, :SG_W]
    vv = gz[:, SG_W:]
    mu = jnp.mean(vv, axis=-1, keepdims=True)
    xc = vv - mu
    var = jnp.mean(xc * xc, axis=-1, keepdims=True)
    vb = (xc * lax.rsqrt(var + EPS) * lng_ref[...] + lnb_ref[...]).astype(BF)
    nch = sub // CHUNK
    low_half = lax.broadcasted_iota(jnp.int32, (CHUNK, LANES), 1) < SG_W // SG_GROUPS
    pair_cols = []
    for j in range(SG_PAIRS):
        rhs = jnp.concatenate(
            [vb[c * CHUNK:(c + 1) * CHUNK, j * LANES:(j + 1) * LANES] for c in range(nch)], axis=1)
        r = jnp.dot(ws_ref[j], rhs, preferred_element_type=F32)
        pair_cols.append(jnp.concatenate(
            [jnp.where(low_half, r[:CHUNK, c * LANES:(c + 1) * LANES], r[CHUNK:, c * LANES:(c + 1) * LANES])
             for c in range(nch)], axis=0))
    bias = jnp.concatenate([bs_ref[...]] * nch, axis=0)
    mixed = jnp.concatenate(pair_cols, axis=1) + bias
    gn_ref[0, rows] = _rms(u * mixed, sgo_ref[...]).astype(BF)


def _proj_call(x, l, p, cos_t, sin_t):
    B, S, _ = x.shape
    tm = min(PROJ_TILE, S)
    n = S // tm
    kc = min(KEY_CHUNK, tm)
    out_shape = (
        jax.ShapeDtypeStruct((B, N_HEADS, KV_W, S), BF),
        jax.ShapeDtypeStruct((B, S, KV_W), BF),
        jax.ShapeDtypeStruct((B, N_KV_HEADS, S), F32),
        jax.ShapeDtypeStruct((B, N_KV_HEADS, S // kc, HEAD_DIM, kc), BF),
        jax.ShapeDtypeStruct((B, S, SG_W), BF),
    )
    return pl.pallas_call(
        _proj_kernel,
        grid=(B, n),
        in_specs=[
            pl.BlockSpec((1, tm, D_MODEL), lambda b, i: (b, i, 0)),
            _layer_spec((1, D_MODEL), l),
            _layer_spec((D_MODEL, D_IN), l),
            _layer_spec((HEAD_DIM, 1), l),
            _layer_spec((HEAD_DIM, 1), l),
            pl.BlockSpec((HEAD_DIM, tm), lambda b, i: (0, i)),
            pl.BlockSpec((HEAD_DIM, tm), lambda b, i: (0, i)),
            _layer_spec((1, SG_W), l),
            _layer_spec((1, SG_W), l),
            _layer_spec((SG_PAIRS, 2 * CHUNK, CHUNK), l),
            _layer_spec((CHUNK, SG_W), l),
            _layer_spec((1, SG_W), l),
        ],
        out_specs=(
            pl.BlockSpec((1, N_HEADS, KV_W, tm), lambda b, i: (b, 0, 0, i)),
            pl.BlockSpec((1, tm, KV_W), lambda b, i: (b, i, 0)),
            pl.BlockSpec((1, N_KV_HEADS, tm), lambda b, i: (b, 0, i)),
            pl.BlockSpec((1, N_KV_HEADS, tm // kc, HEAD_DIM, kc), lambda b, i: (b, 0, i, 0, 0)),
            pl.BlockSpec((1, tm, SG_W), lambda b, i: (b, i, 0)),
        ),
        out_shape=out_shape,
        compiler_params=pltpu.CompilerParams(
            dimension_semantics=("parallel", "parallel"), vmem_limit_bytes=VMEM_LIMIT),
        name="proj",
    )(x, p["pre_mix_g"], p["w_in"], p["q_norm_g"], p["k_norm_g"], cos_t, sin_t, p["sg_norm_g"], p["sg_norm_b"],
      p["sg_w"], p["sg_b"], p["sg_out_g"])


def _attn_kernel(qt_ref, k_ref, ksq_ref, vt_ref, ag_ref, o_ref, dmin_ref, stab_ref, acc_ref, den_ref, sbuf_ref):
    nkv = vt_ref.shape[2]
    tk = vt_ref.shape[4]
    unroll = math.gcd(nkv, KV_UNROLL)

    ik = min(ITEM_KEYS, tk)

    def k_item(j, s):
        start = j * tk + s * ik
        if not isinstance(start, int):
            start = pl.multiple_of(start, ik)
        return k_ref[0, pl.ds(start, ik), :]

    n_pass = nkv // unroll
    pass_items = lambda jj: [(jj * unroll + c, s, h)
                             for c in range(unroll) for s in range(tk // ik) for h in range(N_HEADS)]
    score = lambda j, s, h: jnp.dot(k_item(j, s), qt_ref[0, h], preferred_element_type=F32)

    def first_scores():
        for a, it in enumerate(pass_items(0)[:S_AHEAD]):
            sbuf_ref[a] = score(*it)

    def sweep():
        acc_ref[...] = jnp.zeros(acc_ref.shape, F32)
        den_ref[...] = jnp.zeros(den_ref.shape, F32)

        def body(jj, carry):
            items = pass_items(jj)
            items += pass_items(jnp.minimum(jj + 1, n_pass - 1))[:S_AHEAD]
            ahead = []
            for n, (j, s, h) in enumerate(items[:-S_AHEAD]):
                nxt = score(*items[n + S_AHEAD])
                if n + S_AHEAD < len(items) - S_AHEAD:
                    ahead.append(nxt)
                else:
                    sbuf_ref[n + 2 * S_AHEAD - len(items)] = nxt
                st = sbuf_ref[n] if n < S_AHEAD else ahead.pop(0)
                p = jnp.exp2(st - stab_ref[h])
                den_ref[h] += jnp.sum(p.reshape(ik // SUBLANES, SUBLANES, p.shape[1]), axis=0)
                acc_ref[h] += jnp.dot(vt_ref[0, h // GROUP, j, :, s * ik:(s + 1) * ik], p.astype(BF),
                                      preferred_element_type=F32)
            return carry

        lax.fori_loop(0, n_pass, body, 0)

    first_scores()
    kmax = jnp.sqrt(jnp.max(ksq_ref[0], axis=1, keepdims=True))
    for h in range(N_HEADS):
        q = qt_ref[0, h].astype(F32)
        qn = jnp.sqrt(jnp.sum(q * q, axis=0, keepdims=True))
        stab_ref[h] = qn * kmax[h // GROUP:h // GROUP + 1]
    sweep()

    dens = [jnp.sum(den_ref[h], axis=0, keepdims=True) for h in range(N_HEADS)]
    dmin_ref[0, 0] = functools.reduce(jnp.minimum, dens)
    outs = [acc_ref[h] / dens[h] for h in range(N_HEADS)]
    ot = jnp.concatenate(outs, axis=0)
    ms = jnp.mean(ot * ot, axis=0, keepdims=True)
    o_ref[0] = (ot * lax.rsqrt(ms + EPS) * ag_ref[...]).astype(BF)


def _attn_exact_kernel(qt_ref, k_ref, vt_ref, ag_ref, o_ref, m_ref, l_ref, acc_ref):
    nkv = vt_ref.shape[2]
    tk = vt_ref.shape[4]
    m_ref[...] = jnp.full(m_ref.shape, -jnp.inf, F32)
    l_ref[...] = jnp.zeros(l_ref.shape, F32)
    acc_ref[...] = jnp.zeros(acc_ref.shape, F32)

    def body(j, carry):
        kc = k_ref[0, pl.ds(pl.multiple_of(j * tk, tk), tk), :]
        for h in range(N_HEADS):
            st = jnp.dot(kc, qt_ref[0, h], preferred_element_type=F32)
            m_old = m_ref[h]
            m_new = jnp.maximum(m_old, jnp.max(st, axis=0, keepdims=True))
            alpha = jnp.exp2(m_old - m_new)
            p = jnp.exp2(st - m_new)
            l_ref[h] = alpha * l_ref[h] + jnp.sum(p, axis=0, keepdims=True)
            acc_ref[h] = alpha * acc_ref[h] + jnp.dot(vt_ref[0, h // GROUP, j], p.astype(BF),
                                                      preferred_element_type=F32)
            m_ref[h] = m_new
        return carry

    lax.fori_loop(0, nkv, body, 0)
    ot = jnp.concatenate([acc_ref[h] / l_ref[h] for h in range(N_HEADS)], axis=0)
    ms = jnp.mean(ot * ot, axis=0, keepdims=True)
    o_ref[0] = (ot * lax.rsqrt(ms + EPS) * ag_ref[...]).astype(BF)


def _attn_call(qt, k, ksq, vt, l, p):
    B, _, _, S = qt.shape
    nkv, tk = vt.shape[2], vt.shape[4]
    tq = min(Q_TILE, S)
    qt_spec = pl.BlockSpec((1, N_HEADS, KV_W, tq), lambda b, i: (b, 0, 0, i))
    k_spec = pl.BlockSpec((1, S, KV_W), lambda b, i: (b, 0, 0))
    vt_spec = pl.BlockSpec((1, N_KV_HEADS, nkv, HEAD_DIM, tk), lambda b, i: (b, 0, 0, 0, 0))
    out_spec = pl.BlockSpec((1, ATTN_W, tq), lambda b, i: (b, 0, i))
    out_shape = jax.ShapeDtypeStruct((B, ATTN_W, S), BF)
    params = pltpu.CompilerParams(dimension_semantics=("parallel", "arbitrary"), vmem_limit_bytes=VMEM_LIMIT)

    def exact(_):
        return pl.pallas_call(
            _attn_exact_kernel,
            grid=(B, S // tq),
            in_specs=[qt_spec, k_spec, vt_spec, _layer_spec((ATTN_W, 1), l)],
            out_specs=out_spec,
            out_shape=out_shape,
            scratch_shapes=[
                pltpu.VMEM((N_HEADS, 1, tq), F32),
                pltpu.VMEM((N_HEADS, 1, tq), F32),
                pltpu.VMEM((N_HEADS, HEAD_DIM, tq), F32),
            ],
            compiler_params=params,
            name="attn_exact",
        )(qt, k, vt, p["attn_out_g"])

    an, dmin = _attn_fast_call(qt, k, ksq, vt, l, p)
    return lax.cond(jnp.min(dmin) < MIN_DENOM, exact, lambda _: an, None)


def _attn_fast_call(qt, k, ksq, vt, l, p):
    B, _, _, S = qt.shape
    nkv, tk = vt.shape[2], vt.shape[4]
    tq = min(Q_TILE, S)
    return pl.pallas_call(
        _attn_kernel,
        grid=(B, S // tq),
        in_specs=[
            pl.BlockSpec((1, N_HEADS, KV_W, tq), lambda b, i: (b, 0, 0, i)),
            pl.BlockSpec((1, S, KV_W), lambda b, i: (b, 0, 0)),
            pl.BlockSpec((1, N_KV_HEADS, S), lambda b, i: (b, 0, 0)),
            pl.BlockSpec((1, N_KV_HEADS, nkv, HEAD_DIM, tk), lambda b, i: (b, 0, 0, 0, 0)),
            _layer_spec((ATTN_W, 1), l),
        ],
        out_specs=(
            pl.BlockSpec((1, ATTN_W, tq), lambda b, i: (b, 0, i)),
            pl.BlockSpec((1, 1, 1, tq), lambda b, i: (b, i, 0, 0)),
        ),
        out_shape=(
            jax.ShapeDtypeStruct((B, ATTN_W, S), BF),
            jax.ShapeDtypeStruct((B, S // tq, 1, tq), F32),
        ),
        scratch_shapes=[
            pltpu.VMEM((N_HEADS, 1, tq), F32),
            pltpu.VMEM((N_HEADS, HEAD_DIM, tq), F32),
            pltpu.VMEM((N_HEADS, SUBLANES, tq), F32),
            pltpu.VMEM((S_AHEAD, min(ITEM_KEYS, tk), tq), F32),
        ],
        compiler_params=pltpu.CompilerParams(
            dimension_semantics=("parallel", "arbitrary"), vmem_limit_bytes=VMEM_LIMIT),
        name="attn",
    )(qt, k, ksq, vt, p["attn_out_g"])


def _mix_ffn_kernel(x_ref, an_ref, gn_ref, wo_ref, pmg_ref, pfg_ref, w1_ref, w2_ref, pog_ref, o_ref):
    tm = x_ref.shape[1]
    sub = min(FFN_SUB, tm)
    mixes = []
    for r0 in range(0, tm, sub):
        an = an_ref[0, :, r0:r0 + sub].T
        cat = jnp.concatenate([an, gn_ref[0, r0:r0 + sub]], axis=-1)
        mixes.append(jnp.dot(cat, wo_ref[...], preferred_element_type=F32))
    mids = []
    for i, m in enumerate(mixes):
        x1 = x_ref[0, i * sub:(i + 1) * sub] + _rms(m, pmg_ref[...])
        mids.append((x1, _rms(x1, pfg_ref[...]).astype(BF)))
    for i, (x1, h) in enumerate(mids):
        f = None
        for c in range(D_FF // FF_CHUNK):
            a = jnp.dot(h, w1_ref[:, c * FF_CHUNK:(c + 1) * FF_CHUNK], preferred_element_type=F32)
            a = jnp.maximum(a, 0.0)
            a = (a * a).astype(BF)
            part = jnp.dot(a, w2_ref[c * FF_CHUNK:(c + 1) * FF_CHUNK, :], preferred_element_type=F32)
            f = part if f is None else f + part
        o_ref[0, i * sub:(i + 1) * sub] = x1 + _rms(f, pog_ref[...])


def _mix_ffn_call(x, an, gn, l, p):
    B, S, _ = x.shape
    tm = min(FFN_TILE, S)
    row = lambda width: pl.BlockSpec((1, tm, width), lambda b, i: (b, i, 0))
    return pl.pallas_call(
        _mix_ffn_kernel,
        grid=(B, S // tm),
        in_specs=[
            row(D_MODEL), pl.BlockSpec((1, ATTN_W, tm), lambda b, i: (b, 0, i)), row(SG_W),
            _layer_spec((D_MODEL, D_MODEL), l),
            _layer_spec((1, D_MODEL), l),
            _layer_spec((1, D_MODEL), l),
            _layer_spec((D_MODEL, D_FF), l),
            _layer_spec((D_FF, D_MODEL), l),
            _layer_spec((1, D_MODEL), l),
        ],
        out_specs=row(D_MODEL),
        out_shape=jax.ShapeDtypeStruct((B, S, D_MODEL), F32),
        compiler_params=pltpu.CompilerParams(
            dimension_semantics=("parallel", "parallel"), vmem_limit_bytes=VMEM_LIMIT),
        name="mix_ffn",
    )(x, an, gn, p["w_out"], p["post_mix_g"], p["pre_ffn_g"], p["w_ff1"], p["w_ff2"], p["post_ffn_g"])


def _rope_tables_t(seq_len):
    pos = jnp.arange(seq_len)
    r = (pos // GRID_W).astype(F32)
    c = (pos % GRID_W).astype(F32)
    inv = ROPE_THETA ** (-jnp.arange(0, AXIS_DIM, 2, dtype=F32) / AXIS_DIM)
    ang_r = inv[:, None] * r[None, :]
    ang_c = inv[:, None] * c[None, :]
    ang = jnp.concatenate([ang_r, ang_r, ang_c, ang_c], axis=0)
    half = AXIS_DIM // 2
    sign = jnp.concatenate([-jnp.ones((half, 1), F32), jnp.ones((half, 1), F32)] * 2, axis=0)
    return jnp.cos(ang), jnp.sin(ang) * sign


def _trunk(x, p, cos_t, sin_t):
    for l in range(p["w_in"].shape[0]):
        qt, k, ksq, vt, gn = _proj_call(x, l, p, cos_t, sin_t)
        an = _attn_call(qt, k, ksq, vt, l, p)
        x = _mix_ffn_call(x, an, gn, l, p)
    return x


def kernel(x_prompt, x_sample, w_in, w_out, q_norm_g, k_norm_g, sg_norm_g, sg_norm_b, sg_w, sg_b, attn_out_g,
           sg_out_g, pre_mix_g, post_mix_g, pre_ffn_g, post_ffn_g, w_ff1, w_ff2):
    depth = w_in.shape[0]
    p = dict(
        w_in=w_in.astype(BF), w_out=w_out.astype(BF), w_ff1=w_ff1.astype(BF), w_ff2=w_ff2.astype(BF),
        q_norm_g=q_norm_g[:, :, None], k_norm_g=k_norm_g[:, :, None],
        sg_norm_g=sg_norm_g[:, None, :], sg_norm_b=sg_norm_b[:, None, :],
        sg_w=sg_w.astype(BF).reshape(depth, SG_PAIRS, 2 * CHUNK, CHUNK),
        sg_b=jnp.repeat(jnp.swapaxes(sg_b, 1, 2), SG_W // SG_GROUPS, axis=2),
        attn_out_g=attn_out_g[:, :, None], sg_out_g=sg_out_g[:, None, :], pre_mix_g=pre_mix_g[:, None, :],
        post_mix_g=post_mix_g[:, None, :], pre_ffn_g=pre_ffn_g[:, None, :], post_ffn_g=post_ffn_g[:, None, :],
    )
    cos_t, sin_t = _rope_tables_t(max(x_prompt.shape[1], x_sample.shape[1]))
    return (_trunk(x_prompt, p, cos_t, sin_t), _trunk(x_sample, p, cos_t, sin_t))
```

```python
import functools
import math

import jax
import jax.numpy as jnp
from jax import lax
from jax.experimental import pallas as pl
from jax.experimental.pallas import tpu as pltpu

D_MODEL = 1024
HEAD_DIM = 64
N_HEADS = 8
N_KV_HEADS = 2
GROUP = N_HEADS // N_KV_HEADS
ATTN_W = N_HEADS * HEAD_DIM
KV_W = N_KV_HEADS * HEAD_DIM
QKV_W = ATTN_W + 2 * KV_W
SG_W = 512
SG_GROUPS = 8
SG_PAIRS = SG_GROUPS // 2
CHUNK = 128
GRID_W = 64
AXIS_DIM = HEAD_DIM // 2
ROPE_THETA = 10000.0
D_FF = 4 * D_MODEL
D_IN = QKV_W + 2 * SG_W
EPS = 1e-6

LANES = 128
SUBLANES = 8

PROJ_TILE = 1024
PROJ_SUB = 128
FFN_TILE = 1024
FFN_SUB = 256
FF_CHUNK = 1024
Q_TILE = 256
KEY_CHUNK = 512
ITEM_KEYS = 256
KV_UNROLL = 32
S_AHEAD = 4
LOG2_E = 1.4426950408889634
MIN_DENOM = 2.0 ** -60
VMEM_LIMIT = 56 * 1024 * 1024

BF = jnp.bfloat16
F32 = jnp.float32


def _rms(x, g):
    ms = jnp.mean(x * x, axis=-1, keepdims=True)
    return x * lax.rsqrt(ms + EPS) * g


def _gelu_tanh(x):
    c = 0.7978845608028654
    return 0.5 * x * (1.0 + jnp.tanh(c * (x + 0.044715 * (x * x * x))))


def _layer_spec(shape, l):
    nd = len(shape)
    return pl.BlockSpec((None,) + tuple(shape), lambda *_: (l,) + (0,) * nd, pipeline_mode=pl.Buffered(1))


def _proj_kernel(x_ref, g_ref, w_ref, qg_ref, kg_ref, cos_ref, sin_ref, lng_ref, lnb_ref, ws_ref, bs_ref,
                 sgo_ref, qt_ref, k_ref, ksq_ref, vt_ref, gn_ref):
    tm = x_ref.shape[1]
    sub = min(PROJ_SUB, tm)
    zs = []
    for r0 in range(0, tm, sub):
        h = _rms(x_ref[0, r0:r0 + sub], g_ref[...]).astype(BF)
        zs.append(jnp.dot(h, w_ref[...], preferred_element_type=F32))
    for i, z in enumerate(zs):
        _proj_tail(z, i * sub, sub, qg_ref, kg_ref, cos_ref, sin_ref, lng_ref, lnb_ref, ws_ref, bs_ref, sgo_ref,
                   qt_ref, k_ref, ksq_ref, vt_ref, gn_ref)


def _proj_tail(z, r0, sub, qg_ref, kg_ref, cos_ref, sin_ref, lng_ref, lnb_ref, ws_ref, bs_ref, sgo_ref,
               qt_ref, k_ref, ksq_ref, vt_ref, gn_ref):
    rows = slice(r0, r0 + sub)
    zt = z[:, :QKV_W].T
    cos = cos_ref[:, rows]
    sin = sin_ref[:, rows]

    def norm_rope(blk, gcol):
        ms = jnp.mean(blk * blk, axis=0, keepdims=True)
        y = blk * lax.rsqrt(ms + EPS) * gcol
        half = AXIS_DIM // 2
        sw = jnp.concatenate([y[half:2 * half], y[0:half], y[3 * half:4 * half], y[2 * half:3 * half]], axis=0)
        return y * cos + sw * sin

    zero_blk = jnp.zeros((HEAD_DIM, sub), BF)
    for hh in range(N_HEADS):
        qh = norm_rope(zt[hh * HEAD_DIM:(hh + 1) * HEAD_DIM], qg_ref[...]) * (HEAD_DIM ** -0.5 * LOG2_E)
        c = hh // GROUP
        for cc in range(N_KV_HEADS):
            qt_ref[0, hh, cc * HEAD_DIM:(cc + 1) * HEAD_DIM, rows] = qh.astype(BF) if cc == c else zero_blk
    kt = jnp.concatenate(
        [norm_rope(zt[ATTN_W + c * HEAD_DIM:ATTN_W + (c + 1) * HEAD_DIM], kg_ref[...]) for c in range(N_KV_HEADS)],
        axis=0)
    k_ref[0, rows] = kt.T.astype(BF)
    kf = kt.astype(BF).astype(F32)
    kf = kf * kf
    ksq_ref[0, :, rows] = jnp.concatenate(
        [jnp.sum(kf[c * HEAD_DIM:(c + 1) * HEAD_DIM], axis=0, keepdims=True) for c in range(N_KV_HEADS)], axis=0)
    for c in range(N_KV_HEADS):
        v0 = ATTN_W + KV_W + c * HEAD_DIM
        kc = vt_ref.shape[4]
        vt_ref[0, c, r0 // kc, :, r0 % kc:r0 % kc + sub] = zt[v0:v0 + HEAD_DIM].astype(BF)

    gz = _gelu_tanh(z[:, QKV_W:])
    u = gz[:, :SG_W]
    vv = gz[:, SG_W:]
    mu = jnp.mean(vv, axis=-1, keepdims=True)
    xc = vv - mu
    var = jnp.mean(xc * xc, axis=-1, keepdims=True)
    vb = (xc * lax.rsqrt(var + EPS) * lng_ref[...] + lnb_ref[...]).astype(BF)
    nch = sub // CHUNK
    low_half = lax.broadcasted_iota(jnp.int32, (CHUNK, LANES), 1) < SG_W // SG_GROUPS
    pair_cols = []
    for j in range(SG_PAIRS):
        rhs = jnp.concatenate(
            [vb[c * CHUNK:(c + 1) * CHUNK, j * LANES:(j + 1) * LANES] for c in range(nch)], axis=1)
        r = jnp.dot(ws_ref[j], rhs, preferred_element_type=F32)
        pair_cols.append(jnp.concatenate(
            [jnp.where(low_half, r[:CHUNK, c * LANES:(c + 1) * LANES], r[CHUNK:, c * LANES:(c + 1) * LANES])
             for c in range(nch)], axis=0))
    bias = jnp.concatenate([bs_ref[...]] * nch, axis=0)
    mixed = jnp.concatenate(pair_cols, axis=1) + bias
    gn_ref[0, rows] = _rms(u * mixed, sgo_ref[...]).astype(BF)


def _proj_call(x, l, p, cos_t, sin_t):
    B, S, _ = x.shape
    tm = min(PROJ_TILE, S)
    n = S // tm
    kc = min(KEY_CHUNK, tm)
    out_shape = (
        jax.ShapeDtypeStruct((B, N_HEADS, KV_W, S), BF),
        jax.ShapeDtypeStruct((B, S, KV_W), BF),
        jax.ShapeDtypeStruct((B, N_KV_HEADS, S), F32),
        jax.ShapeDtypeStruct((B, N_KV_HEADS, S // kc, HEAD_DIM, kc), BF),
        jax.ShapeDtypeStruct((B, S, SG_W), BF),
    )
    return pl.pallas_call(
        _proj_kernel,
        grid=(B, n),
        in_specs=[
            pl.BlockSpec((1, tm, D_MODEL), lambda b, i: (b, i, 0)),
            _layer_spec((1, D_MODEL), l),
            _layer_spec((D_MODEL, D_IN), l),
            _layer_spec((HEAD_DIM, 1), l),
            _layer_spec((HEAD_DIM, 1), l),
            pl.BlockSpec((HEAD_DIM, tm), lambda b, i: (0, i)),
            pl.BlockSpec((HEAD_DIM, tm), lambda b, i: (0, i)),
            _layer_spec((1, SG_W), l),
            _layer_spec((1, SG_W), l),
            _layer_spec((SG_PAIRS, 2 * CHUNK, CHUNK), l),
            _layer_spec((CHUNK, SG_W), l),
            _layer_spec((1, SG_W), l),
        ],
        out_specs=(
            pl.BlockSpec((1, N_HEADS, KV_W, tm), lambda b, i: (b, 0, 0, i)),
            pl.BlockSpec((1, tm, KV_W), lambda b, i: (b, i, 0)),
            pl.BlockSpec((1, N_KV_HEADS, tm), lambda b, i: (b, 0, i)),
            pl.BlockSpec((1, N_KV_HEADS, tm // kc, HEAD_DIM, kc), lambda b, i: (b, 0, i, 0, 0)),
            pl.BlockSpec((1, tm, SG_W), lambda b, i: (b, i, 0)),
        ),
        out_shape=out_shape,
        compiler_params=pltpu.CompilerParams(
            dimension_semantics=("parallel", "parallel"), vmem_limit_bytes=VMEM_LIMIT),
        name="proj",
    )(x, p["pre_mix_g"], p["w_in"], p["q_norm_g"], p["k_norm_g"], cos_t, sin_t, p["sg_norm_g"], p["sg_norm_b"],
      p["sg_w"], p["sg_b"], p["sg_out_g"])


def _attn_kernel(qt_ref, k_ref, ksq_ref, vt_ref, ag_ref, o_ref, dmin_ref, stab_ref, acc_ref, den_ref):
    nkv = vt_ref.shape[2]
    tk = vt_ref.shape[4]
    unroll = math.gcd(nkv, KV_UNROLL)

    ik = min(ITEM_KEYS, tk)

    def k_item(j, s):
        start = j * tk + s * ik
        if not isinstance(start, int):
            start = pl.multiple_of(start, ik)
        return k_ref[0, pl.ds(start, ik), :]

    n_pass = nkv // unroll
    pass_items = lambda jj: [(jj * unroll + c, s, h)
                             for c in range(unroll) for s in range(tk // ik) for h in range(N_HEADS)]
    score = lambda j, s, h: jnp.dot(k_item(j, s), qt_ref[0, h], preferred_element_type=F32)

    def sweep():
        acc_ref[...] = jnp.zeros(acc_ref.shape, F32)
        den_ref[...] = jnp.zeros(den_ref.shape, F32)

        def body(jj, carry):
            items = pass_items(jj)
            ahead = [score(*it) for it in items[:S_AHEAD]]
            for n, (j, s, h) in enumerate(items):
                if n + S_AHEAD < len(items):
                    ahead.append(score(*items[n + S_AHEAD]))
                st = ahead.pop(0)
                p = jnp.exp2(st - stab_ref[h])
                den_ref[h] += jnp.sum(p.reshape(ik // SUBLANES, SUBLANES, p.shape[1]), axis=0)
                acc_ref[h] += jnp.dot(vt_ref[0, h // GROUP, j, :, s * ik:(s + 1) * ik], p.astype(BF),
                                      preferred_element_type=F32)
            return carry

        if n_pass == 1:
            body(0, 0)
        else:
            lax.fori_loop(0, n_pass, body, 0)

    kmax = jnp.sqrt(jnp.max(ksq_ref[0], axis=1, keepdims=True))
    for h in range(N_HEADS):
        q = qt_ref[0, h].astype(F32)
        qn = jnp.sqrt(jnp.sum(q * q, axis=0, keepdims=True))
        stab_ref[h] = qn * kmax[h // GROUP:h // GROUP + 1]
    sweep()

    dens = [jnp.sum(den_ref[h], axis=0, keepdims=True) for h in range(N_HEADS)]
    dmin_ref[0, 0] = functools.reduce(jnp.minimum, dens)
    outs = [acc_ref[h] / dens[h] for h in range(N_HEADS)]
    ot = jnp.concatenate(outs, axis=0)
    ms = jnp.mean(ot * ot, axis=0, keepdims=True)
    o_ref[0] = (ot * lax.rsqrt(ms + EPS) * ag_ref[...]).astype(BF)


def _attn_exact_kernel(qt_ref, k_ref, vt_ref, ag_ref, o_ref, m_ref, l_ref, acc_ref):
    nkv = vt_ref.shape[2]
    tk = vt_ref.shape[4]
    m_ref[...] = jnp.full(m_ref.shape, -jnp.inf, F32)
    l_ref[...] = jnp.zeros(l_ref.shape, F32)
    acc_ref[...] = jnp.zeros(acc_ref.shape, F32)

    def body(j, carry):
        kc = k_ref[0, pl.ds(pl.multiple_of(j * tk, tk), tk), :]
        for h in range(N_HEADS):
            st = jnp.dot(kc, qt_ref[0, h], preferred_element_type=F32)
            m_old = m_ref[h]
            m_new = jnp.maximum(m_old, jnp.max(st, axis=0, keepdims=True))
            alpha = jnp.exp2(m_old - m_new)
            p = jnp.exp2(st - m_new)
            l_ref[h] = alpha * l_ref[h] + jnp.sum(p, axis=0, keepdims=True)
            acc_ref[h] = alpha * acc_ref[h] + jnp.dot(vt_ref[0, h // GROUP, j], p.astype(BF),
                                                      preferred_element_type=F32)
            m_ref[h] = m_new
        return carry

    lax.fori_loop(0, nkv, body, 0)
    ot = jnp.concatenate([acc_ref[h] / l_ref[h] for h in range(N_HEADS)], axis=0)
    ms = jnp.mean(ot * ot, axis=0, keepdims=True)
    o_ref[0] = (ot * lax.rsqrt(ms + EPS) * ag_ref[...]).astype(BF)


def _attn_call(qt, k, ksq, vt, l, p):
    B, _, _, S = qt.shape
    nkv, tk = vt.shape[2], vt.shape[4]
    tq = min(Q_TILE, S)
    qt_spec = pl.BlockSpec((1, N_HEADS, KV_W, tq), lambda b, i: (b, 0, 0, i))
    k_spec = pl.BlockSpec((1, S, KV_W), lambda b, i: (b, 0, 0))
    vt_spec = pl.BlockSpec((1, N_KV_HEADS, nkv, HEAD_DIM, tk), lambda b, i: (b, 0, 0, 0, 0))
    out_spec = pl.BlockSpec((1, ATTN_W, tq), lambda b, i: (b, 0, i))
    out_shape = jax.ShapeDtypeStruct((B, ATTN_W, S), BF)
    params = pltpu.CompilerParams(dimension_semantics=("parallel", "arbitrary"), vmem_limit_bytes=VMEM_LIMIT)

    def exact(_):
        return pl.pallas_call(
            _attn_exact_kernel,
            grid=(B, S // tq),
            in_specs=[qt_spec, k_spec, vt_spec, _layer_spec((ATTN_W, 1), l)],
            out_specs=out_spec,
            out_shape=out_shape,
            scratch_shapes=[
                pltpu.VMEM((N_HEADS, 1, tq), F32),
                pltpu.VMEM((N_HEADS, 1, tq), F32),
                pltpu.VMEM((N_HEADS, HEAD_DIM, tq), F32),
            ],
            compiler_params=params,
            name="attn_exact",
        )(qt, k, vt, p["attn_out_g"])

    an, dmin = _attn_fast_call(qt, k, ksq, vt, l, p)
    return lax.cond(jnp.min(dmin) < MIN_DENOM, exact, lambda _: an, None)


def _attn_fast_call(qt, k, ksq, vt, l, p):
    B, _, _, S = qt.shape
    nkv, tk = vt.shape[2], vt.shape[4]
    tq = min(Q_TILE, S)
    return pl.pallas_call(
        _attn_kernel,
        grid=(B, S // tq),
        in_specs=[
            pl.BlockSpec((1, N_HEADS, KV_W, tq), lambda b, i: (b, 0, 0, i)),
            pl.BlockSpec((1, S, KV_W), lambda b, i: (b, 0, 0)),
            pl.BlockSpec((1, N_KV_HEADS, S), lambda b, i: (b, 0, 0)),
            pl.BlockSpec((1, N_KV_HEADS, nkv, HEAD_DIM, tk), lambda b, i: (b, 0, 0, 0, 0)),
            _layer_spec((ATTN_W, 1), l),
        ],
        out_specs=(
            pl.BlockSpec((1, ATTN_W, tq), lambda b, i: (b, 0, i)),
            pl.BlockSpec((1, 1, 1, tq), lambda b, i: (b, i, 0, 0)),
        ),
        out_shape=(
            jax.ShapeDtypeStruct((B, ATTN_W, S), BF),
            jax.ShapeDtypeStruct((B, S // tq, 1, tq), F32),
        ),
        scratch_shapes=[
            pltpu.VMEM((N_HEADS, 1, tq), F32),
            pltpu.VMEM((N_HEADS, HEAD_DIM, tq), F32),
            pltpu.VMEM((N_HEADS, SUBLANES, tq), F32),
        ],
        compiler_params=pltpu.CompilerParams(
            dimension_semantics=("parallel", "arbitrary"), vmem_limit_bytes=VMEM_LIMIT),
        name="attn",
    )(qt, k, ksq, vt, p["attn_out_g"])


def _mix_ffn_kernel(x_ref, an_ref, gn_ref, wo_ref, pmg_ref, pfg_ref, w1_ref, w2_ref, pog_ref, o_ref):
    tm = x_ref.shape[1]
    sub = min(FFN_SUB, tm)
    mixes = []
    for r0 in range(0, tm, sub):
        an = an_ref[0, :, r0:r0 + sub].T
        cat = jnp.concatenate([an, gn_ref[0, r0:r0 + sub]], axis=-1)
        mixes.append(jnp.dot(cat, wo_ref[...], preferred_element_type=F32))
    mids = []
    for i, m in enumerate(mixes):
        x1 = x_ref[0, i * sub:(i + 1) * sub] + _rms(m, pmg_ref[...])
        mids.append((x1, _rms(x1, pfg_ref[...]).astype(BF)))
    for i, (x1, h) in enumerate(mids):
        f = None
        for c in range(D_FF // FF_CHUNK):
            a = jnp.dot(h, w1_ref[:, c * FF_CHUNK:(c + 1) * FF_CHUNK], preferred_element_type=F32)
            a = jnp.maximum(a, 0.0)
            a = (a * a).astype(BF)
            part = jnp.dot(a, w2_ref[c * FF_CHUNK:(c + 1) * FF_CHUNK, :], preferred_element_type=F32)
            f = part if f is None else f + part
        o_ref[0, i * sub:(i + 1) * sub] = x1 + _rms(f, pog_ref[...])


def _mix_ffn_call(x, an, gn, l, p):
    B, S, _ = x.shape
    tm = min(FFN_TILE, S)
    row = lambda width: pl.BlockSpec((1, tm, width), lambda b, i: (b, i, 0))
    return pl.pallas_call(
        _mix_ffn_kernel,
        grid=(B, S // tm),
        in_specs=[
            row(D_MODEL), pl.BlockSpec((1, ATTN_W, tm), lambda b, i: (b, 0, i)), row(SG_W),
            _layer_spec((D_MODEL, D_MODEL), l),
            _layer_spec((1, D_MODEL), l),
            _layer_spec((1, D_MODEL), l),
            _layer_spec((D_MODEL, D_FF), l),
            _layer_spec((D_FF, D_MODEL), l),
            _layer_spec((1, D_MODEL), l),
        ],
        out_specs=row(D_MODEL),
        out_shape=jax.ShapeDtypeStruct((B, S, D_MODEL), F32),
        compiler_params=pltpu.CompilerParams(
            dimension_semantics=("parallel", "parallel"), vmem_limit_bytes=VMEM_LIMIT),
        name="mix_ffn",
    )(x, an, gn, p["w_out"], p["post_mix_g"], p["pre_ffn_g"], p["w_ff1"], p["w_ff2"], p["post_ffn_g"])


def _rope_tables_t(seq_len):
    pos = jnp.arange(seq_len)
    r = (pos // GRID_W).astype(F32)
    c = (pos % GRID_W).astype(F32)
    inv = ROPE_THETA ** (-jnp.arange(0, AXIS_DIM, 2, dtype=F32) / AXIS_DIM)
    ang_r = inv[:, None] * r[None, :]
    ang_c = inv[:, None] * c[None, :]
    ang = jnp.concatenate([ang_r, ang_r, ang_c, ang_c], axis=0)
    half = AXIS_DIM // 2
    sign = jnp.concatenate([-jnp.ones((half, 1), F32), jnp.ones((half, 1), F32)] * 2, axis=0)
    return jnp.cos(ang), jnp.sin(ang) * sign


def _trunk(x, p, cos_t, sin_t):
    for l in range(p["w_in"].shape[0]):
        qt, k, ksq, vt, gn = _proj_call(x, l, p, cos_t, sin_t)
        an = _attn_call(qt, k, ksq, vt, l, p)
        x = _mix_ffn_call(x, an, gn, l, p)
    return x


def kernel(x_prompt, x_sample, w_in, w_out, q_norm_g, k_norm_g, sg_norm_g, sg_norm_b, sg_w, sg_b, attn_out_g,
           sg_out_g, pre_mix_g, post_mix_g, pre_ffn_g, post_ffn_g, w_ff1, w_ff2):
    depth = w_in.shape[0]
    p = dict(
        w_in=w_in.astype(BF), w_out=w_out.astype(BF), w_ff1=w_ff1.astype(BF), w_ff2=w_ff2.astype(BF),
        q_norm_g=q_norm_g[:, :, None], k_norm_g=k_norm_g[:, :, None],
        sg_norm_g=sg_norm_g[:, None, :], sg_norm_b=sg_norm_b[:, None, :],
        sg_w=sg_w.astype(BF).reshape(depth, SG_PAIRS, 2 * CHUNK, CHUNK),
        sg_b=jnp.repeat(jnp.swapaxes(sg_b, 1, 2), SG_W // SG_GROUPS, axis=2),
        attn_out_g=attn_out_g[:, :, None], sg_out_g=sg_out_g[:, None, :], pre_mix_g=pre_mix_g[:, None, :],
        post_mix_g=post_mix_g[:, None, :], pre_ffn_g=pre_ffn_g[:, None, :], post_ffn_g=post_ffn_g[:, None, :],
    )
    cos_t, sin_t = _rope_tables_t(max(x_prompt.shape[1], x_sample.shape[1]))
    return (_trunk(x_prompt, p, cos_t, sin_t), _trunk(x_sample, p, cos_t, sin_t))
```
